```python
import math
import jax, jax.numpy as jnp
from jax import lax
import numpy as np

D_MODEL = 1024
BATCH = 4
SEQ = 8192
DEPTH = 1
DEC_BATCH = 32
DEC_SEQ = 4
PAST_LEN = 16384
PAGE_SIZE = 128

N_HEADS = 8
HEAD_DIM = 64
V_DIM = 2 * HEAD_DIM
QK_WIDTH = N_HEADS * 2 * HEAD_DIM
ATTN_WIDTH = N_HEADS * V_DIM
ROT_DIM = HEAD_DIM // 4
ROPE_THETA = 500000.0
Q_BLOCK = 128
CONV_CH = D_MODEL // 2
CONV_K = 3
N_EXPERTS = 32
TOP_K = 4
D_FF = D_MODEL
SWIGLU_LIMIT = 7.0
SWIGLU_ALPHA = 1.702
MOE_BLOCK = 128
ALPHA = (2.0 * DEPTH) ** 0.25
BETA = (8.0 * DEPTH) ** -0.25
LN_EPS = 1e-5
IN_SPLIT_SIZES = [QK_WIDTH, QK_WIDTH, ATTN_WIDTH, CONV_CH, CONV_CH, CONV_CH, D_MODEL, D_MODEL]
IN_SPLIT_POINTS = [int(v) for v in np.cumsum(IN_SPLIT_SIZES)[:-1]]
IN_COLS = int(sum(IN_SPLIT_SIZES))
V_OFF = 2 * QK_WIDTH

kernel_name = "hybrid_diffattn_shortconv_moe_step"


def layer_norm(x, g, b):
    xf = x.astype(jnp.float32)
    mu = jnp.mean(xf, axis=-1, keepdims=True)
    var = jnp.mean(jnp.square(xf - mu), axis=-1, keepdims=True)
    return ((xf - mu) * lax.rsqrt(var + LN_EPS) * g + b).astype(x.dtype)


def rope_partial(x, pos):
    inv = ROPE_THETA ** (-jnp.arange(0, ROT_DIM, 2, dtype=jnp.float32) / ROT_DIM)
    ang = pos.astype(jnp.float32)[:, None] * inv[None, :]
    cos = jnp.cos(ang)[:, None, None, :]
    sin = jnp.sin(ang)[:, None, None, :]
    half = ROT_DIM // 2
    x1 = x[..., :half].astype(jnp.float32)
    x2 = x[..., half:ROT_DIM].astype(jnp.float32)
    rot = jnp.concatenate([x1 * cos - x2 * sin, x2 * cos + x1 * sin], axis=-1).astype(x.dtype)
    return jnp.concatenate([rot, x[..., ROT_DIM:]], axis=-1)


def prompt_diff_attention(q, k, v, lam):
    B, S = q.shape[0], q.shape[1]
    nqb = S // Q_BLOCK
    qb = q.reshape(B, nqb, Q_BLOCK, N_HEADS, 2, HEAD_DIM).swapaxes(0, 1)
    kpos = jnp.arange(S)
    scale = HEAD_DIM ** -0.5

    def block(args):
        qi, i = args
        s = jnp.einsum('bqhcd,bkhcd->bhcqk', qi, k).astype(jnp.float32) * scale
        qpos = i * Q_BLOCK + jnp.arange(Q_BLOCK)
        s = jnp.where(kpos[None, :] <= qpos[:, None], s, -jnp.inf)
        p = jax.nn.softmax(s, axis=-1)
        w = p[:, :, 0] - lam * p[:, :, 1]
        return jnp.einsum('bhqk,bkhv->bqhv', w.astype(v.dtype), v)

    o = lax.map(block, (qb, jnp.arange(nqb)))
    return o.swapaxes(0, 1).reshape(B, S, N_HEADS, V_DIM)


def sample_diff_attention(q, k, v, k_past, v_past, lam):
    T = q.shape[1]
    P = k_past.shape[1]
    scale = HEAD_DIM ** -0.5
    s_past = jnp.einsum('bqhcd,bkhcd->bhcqk', q, k_past).astype(jnp.float32) * scale
    s_new = jnp.einsum('bqhcd,bkhcd->bhcqk', q, k).astype(jnp.float32) * scale
    causal = jnp.arange(T)[None, :] <= jnp.arange(T)[:, None]
    s_new = jnp.where(causal, s_new, -jnp.inf)
    p = jax.nn.softmax(jnp.concatenate([s_past, s_new], axis=-1), axis=-1)
    w = (p[:, :, 0] - lam * p[:, :, 1]).astype(v.dtype)
    return (jnp.einsum('bhqk,bkhv->bqhv', w[..., :P], v_past)
            + jnp.einsum('bhqk,bkhv->bqhv', w[..., P:], v))


def head_out(o, gain, lam_init):
    of = o.astype(jnp.float32)
    of = of * lax.rsqrt(jnp.mean(jnp.square(of), axis=-1, keepdims=True) + LN_EPS) * gain * (1.0 - lam_init)
    return of.reshape(o.shape[0], o.shape[1], ATTN_WIDTH).astype(o.dtype)


def short_conv(bg, cg, xc, conv_state, w_conv, b_conv):
    u = cg * xc
    u_ext = jnp.concatenate([conv_state.astype(u.dtype), u], axis=1)
    S = u.shape[1]
    y = b_conv + sum(u_ext[:, j:j + S] * w_conv[j] for j in range(CONV_K))
    return bg * y, u_ext[:, -(CONV_K - 1):]


def moe(x, w_router, b_router, w_gate_up, b_gate_up, w_down, b_down):
    N, D = x.shape
    logits = (x @ w_router).astype(jnp.float32) + b_router
    top_v, top_i = lax.top_k(logits, TOP_K)
    gate = jax.nn.softmax(top_v, axis=-1).astype(x.dtype)
    A = N * TOP_K
    flat_e = top_i.reshape(A)
    order = jnp.argsort(flat_e)
    sorted_e = flat_e[order]
    tok = order // TOP_K
    counts = jnp.zeros((N_EXPERTS,), jnp.int32).at[flat_e].add(1)
    padded = (counts + MOE_BLOCK - 1) // MOE_BLOCK * MOE_BLOCK
    pad_end = jnp.cumsum(padded)
    pad_start = pad_end - padded
    start = jnp.cumsum(counts) - counts
    dest = pad_start[sorted_e] + jnp.arange(A) - start[sorted_e]
    n_blocks = (A + N_EXPERTS * (MOE_BLOCK - 1) + MOE_BLOCK - 1) // MOE_BLOCK
    slot_tok = jnp.full((n_blocks * MOE_BLOCK,), N, jnp.int32).at[dest].set(tok)
    x_pad = jnp.concatenate([x, jnp.zeros((1, D), x.dtype)], axis=0)
    xb = x_pad[slot_tok].reshape(n_blocks, MOE_BLOCK, D)
    block_e = jnp.minimum(jnp.searchsorted(pad_end, jnp.arange(n_blocks) * MOE_BLOCK, side='right'),
                          N_EXPERTS - 1)

    def expert_block(args):
        xi, e = args
        h = xi @ w_gate_up[e] + b_gate_up[e]
        g = jnp.minimum(h[..., 0::2], SWIGLU_LIMIT)
        u = jnp.clip(h[..., 1::2], -SWIGLU_LIMIT, SWIGLU_LIMIT)
        act = (u + 1.0) * g * jax.nn.sigmoid(SWIGLU_ALPHA * g)
        return act @ w_down[e] + b_down[e]

    yb = lax.map(expert_block, (xb, block_e)).reshape(n_blocks * MOE_BLOCK, D)
    y_assign = yb[dest] * gate.reshape(A)[order][:, None]
    return jax.ops.segment_sum(y_assign, tok, num_segments=N)


def decoder_layer(x, pos, conv_state, attn_fn, lam_init, p):
    B, S, D = x.shape
    h = x @ p['w_in']
    q, k, v, bg, cg, xc, ga, gc = jnp.split(h, IN_SPLIT_POINTS, axis=-1)
    q = rope_partial(q.reshape(B, S, N_HEADS, 2, HEAD_DIM), pos)
    k = rope_partial(k.reshape(B, S, N_HEADS, 2, HEAD_DIM), pos)
    v = v.reshape(B, S, N_HEADS, V_DIM)
    y_a = head_out(attn_fn(q, k, v), p['subln_g'], lam_init) @ p['w_attn_out']
    y_c, new_conv = short_conv(bg, cg, xc, conv_state, p['w_conv'], p['b_conv'])
    y_c = y_c @ p['w_conv_out']
    mix = (jax.nn.sigmoid(ga) * y_a + jax.nn.sigmoid(gc) * y_c) @ p['w_o']
    x = layer_norm(ALPHA * x + mix, p['ln1_g'], p['ln1_b'])
    f = moe(x.reshape(B * S, D), p['w_router'], p['b_router'], p['w_gate_up'], p['b_gate_up'],
            p['w_down'], p['b_down']).reshape(B, S, D)
    x = layer_norm(ALPHA * x + f, p['ln2_g'], p['ln2_b'])
    return x, k, v, new_conv


def setup_inputs(seed: int = 0) -> dict:
    key = jax.random.key(seed)
    ks = jax.random.split(key, 32)
    f32 = jnp.float32

    def nrm(k, shape, scale):
        return jax.random.normal(k, shape, f32) * scale

    n_pages = PAST_LEN // PAGE_SIZE
    n_used = DEC_BATCH * n_pages
    n_phys = n_used + n_used // 4
    perm = jax.random.permutation(ks[0], n_phys)
    page_table = perm[:n_used].reshape(DEC_BATCH, n_pages).astype(jnp.int32)

    w_in = nrm(ks[6], (DEPTH, D_MODEL, IN_COLS), D_MODEL ** -0.5)
    w_in = w_in.at[:, :, V_OFF:V_OFF + ATTN_WIDTH].multiply(BETA)
    return {
        'x_prompt': nrm(ks[1], (BATCH, SEQ, D_MODEL), 1.0),
        'x_sample': nrm(ks[2], (DEC_BATCH, DEC_SEQ, D_MODEL), 1.0),
        'cache_k': nrm(ks[3], (DEPTH, n_phys, PAGE_SIZE, N_HEADS, 2, HEAD_DIM), 1.0),
        'cache_v': nrm(ks[4], (DEPTH, n_phys, PAGE_SIZE, N_HEADS, V_DIM), 1.0),
        'state_conv': nrm(ks[5], (DEPTH, DEC_BATCH, CONV_K - 1, CONV_CH), 1.0),
        'page_table': page_table,
        'w_in': w_in,
        'lambda_q1': nrm(ks[7], (DEPTH, HEAD_DIM), 0.1),
        'lambda_k1': nrm(ks[8], (DEPTH, HEAD_DIM), 0.1),
        'lambda_q2': nrm(ks[9], (DEPTH, HEAD_DIM), 0.1),
        'lambda_k2': nrm(ks[10], (DEPTH, HEAD_DIM), 0.1),
        'subln_g': 1.0 + nrm(ks[11], (DEPTH, V_DIM), 0.02),
        'w_attn_out': nrm(ks[12], (DEPTH, ATTN_WIDTH, D_MODEL), ATTN_WIDTH ** -0.5),
        'w_conv': nrm(ks[13], (DEPTH, CONV_K, CONV_CH), CONV_K ** -0.5),
        'b_conv': nrm(ks[14], (DEPTH, CONV_CH), 0.02),
        'w_conv_out': nrm(ks[15], (DEPTH, CONV_CH, D_MODEL), CONV_CH ** -0.5),
        'w_o': nrm(ks[16], (DEPTH, D_MODEL, D_MODEL), BETA * D_MODEL ** -0.5),
        'ln1_g': 1.0 + nrm(ks[17], (DEPTH, D_MODEL), 0.02),
        'ln1_b': nrm(ks[18], (DEPTH, D_MODEL), 0.02),
        'w_router': nrm(ks[19], (DEPTH, D_MODEL, N_EXPERTS), D_MODEL ** -0.5),
        'b_router': nrm(ks[20], (DEPTH, N_EXPERTS), 0.01),
        'w_gate_up': nrm(ks[21], (DEPTH, N_EXPERTS, D_MODEL, 2 * D_FF), D_MODEL ** -0.5),
        'b_gate_up': nrm(ks[22], (DEPTH, N_EXPERTS, 2 * D_FF), 0.02),
        'w_down': nrm(ks[23], (DEPTH, N_EXPERTS, D_FF, D_MODEL), BETA * D_FF ** -0.5),
        'b_down': nrm(ks[24], (DEPTH, N_EXPERTS, D_MODEL), 0.02),
        'ln2_g': 1.0 + nrm(ks[25], (DEPTH, D_MODEL), 0.02),
        'ln2_b': nrm(ks[26], (DEPTH, D_MODEL), 0.02),
    }


def reference(x_prompt, x_sample, cache_k, cache_v, state_conv, page_table, w_in, lambda_q1, lambda_k1,
              lambda_q2, lambda_k2, subln_g, w_attn_out, w_conv, b_conv, w_conv_out, w_o, ln1_g, ln1_b,
              w_router, b_router, w_gate_up, b_gate_up, w_down, b_down, ln2_g, ln2_b):
    Bp, S, _ = x_prompt.shape
    Bd, T, _ = x_sample.shape
    past = page_table.shape[1] * PAGE_SIZE
    pos_p = jnp.arange(S)
    pos_s = past + jnp.arange(T)
    hp, hs = x_prompt, x_sample
    kp_l, vp_l, cp_l, ks_l, vs_l, cs_l = [], [], [], [], [], []
    for l in range(DEPTH):
        lam_init = 0.8 - 0.6 * math.exp(-0.3 * l)
        lam = (jnp.exp(jnp.sum(lambda_q1[l] * lambda_k1[l]).astype(jnp.float32))
               - jnp.exp(jnp.sum(lambda_q2[l] * lambda_k2[l]).astype(jnp.float32)) + lam_init)
        p = {'w_in': w_in[l], 'subln_g': subln_g[l], 'w_attn_out': w_attn_out[l], 'w_conv': w_conv[l],
             'b_conv': b_conv[l], 'w_conv_out': w_conv_out[l], 'w_o': w_o[l], 'ln1_g': ln1_g[l],
             'ln1_b': ln1_b[l], 'w_router': w_router[l], 'b_router': b_router[l], 'w_gate_up': w_gate_up[l],
             'b_gate_up': b_gate_up[l], 'w_down': w_down[l], 'b_down': b_down[l], 'ln2_g': ln2_g[l],
             'ln2_b': ln2_b[l]}
        conv0 = jnp.zeros((Bp, CONV_K - 1, CONV_CH), x_prompt.dtype)
        hp, k_p, v_p, c_p = decoder_layer(
            hp, pos_p, conv0, lambda q, k, v: prompt_diff_attention(q, k, v, lam), lam_init, p)
        k_past = cache_k[l, page_table].reshape(Bd, past, N_HEADS, 2, HEAD_DIM)
        v_past = cache_v[l, page_table].reshape(Bd, past, N_HEADS, V_DIM)
        hs, k_s, v_s, c_s = decoder_layer(
            hs, pos_s, state_conv[l],
            lambda q, k, v: sample_diff_attention(q, k, v, k_past, v_past, lam), lam_init, p)
        kp_l.append(k_p); vp_l.append(v_p); cp_l.append(c_p)
        ks_l.append(k_s); vs_l.append(v_s); cs_l.append(c_s)
    new_k_prompt = jnp.stack(kp_l)
    new_v_prompt = jnp.stack(vp_l)
    new_conv_prompt = jnp.stack(cp_l)
    new_k_sample = jnp.stack(ks_l)
    new_v_sample = jnp.stack(vs_l)
    new_conv_sample = jnp.stack(cs_l)
    return (hp, hs, new_k_prompt, new_v_prompt, new_conv_prompt, new_k_sample, new_v_sample, new_conv_sample)
```

```python
import functools
import math

import jax
import jax.numpy as jnp
from jax import lax
from jax.experimental import pallas as pl
from jax.experimental.pallas import tpu as pltpu

F32 = jnp.float32
BF16 = jnp.bfloat16

D_MODEL = 1024
N_HEADS = 8
HEAD_DIM = 64
V_DIM = 2 * HEAD_DIM
QK_WIDTH = N_HEADS * 2 * HEAD_DIM
ATTN_WIDTH = N_HEADS * V_DIM
ROT_DIM = HEAD_DIM // 4
ROPE_THETA = 500000.0
CONV_CH = D_MODEL // 2
CONV_K = 3
N_EXPERTS = 32
TOP_K = 4
D_FF = D_MODEL
SWIGLU_LIMIT = 7.0
SWIGLU_ALPHA = 1.702
LN_EPS = 1e-5
PAGE_SIZE = 128
LANES = 128
MIB = 1024 * 1024

QKV_COLS = 2 * QK_WIDTH + ATTN_WIDTH
REST_COLS = 3 * CONV_CH + 2 * D_MODEL

PAGES_PER_STEP = 8
FFN_TILE = 512


def _params(semantics, vmem_mib):
    return pltpu.CompilerParams(dimension_semantics=semantics, vmem_limit_bytes=vmem_mib * MIB)


def _dot(a, b):
    return jnp.dot(a, b, preferred_element_type=F32)


def _dot_nt(a, b):
    return lax.dot_general(a, b, (((1,), (1,)), ((), ())), preferred_element_type=F32)


def _rope_tables(pos):
    inv = ROPE_THETA ** (-jnp.arange(0, ROT_DIM, 2, dtype=F32) / ROT_DIM)
    ang = pos.astype(F32)[:, None] * inv[None, :]
    cos, sin = jnp.cos(ang), jnp.sin(ang)
    n = pos.shape[0]
    half = ROT_DIM // 2
    rest = HEAD_DIM - ROT_DIM
    c = jnp.concatenate([cos, cos, jnp.ones((n, rest), F32)], axis=1)
    s_up = jnp.concatenate([-sin, jnp.zeros((n, half + rest), F32)], axis=1)
    s_dn = jnp.concatenate([jnp.zeros((n, half), F32), sin, jnp.zeros((n, rest), F32)], axis=1)
    rep = LANES // HEAD_DIM
    return tuple(jnp.tile(t, (1, rep)) for t in (c, s_up, s_dn))


def _qkv_kernel(x_ref, w_ref, c_ref, su_ref, sd_ref, q1_ref, q2_ref, kf_ref, kb_ref, vf_ref, vb_ref):
    x = x_ref[...].astype(BF16)
    c, s_up, s_dn = c_ref[...], su_ref[...], sd_ref[...]
    half = ROT_DIM // 2
    first_map = lax.broadcasted_iota(jnp.int32, c.shape, 1) < HEAD_DIM
    scale = HEAD_DIM ** -0.5

    def rope(blk):
        return blk * c + pltpu.roll(blk, LANES - half, 1) * s_up + pltpu.roll(blk, half, 1) * s_dn

    hq = _dot(x, w_ref[:, 0:QK_WIDTH])
    for j in range(QK_WIDTH // LANES):
        sl = slice(LANES * j, LANES * (j + 1))
        r = rope(hq[:, sl]) * scale
        q1_ref[:, sl] = jnp.where(first_map, r, 0.0).astype(BF16)
        q2_ref[:, sl] = jnp.where(first_map, 0.0, r).astype(BF16)
    hk = _dot(x, w_ref[:, QK_WIDTH:2 * QK_WIDTH])
    for j in range(QK_WIDTH // LANES):
        sl = slice(LANES * j, LANES * (j + 1))
        r = rope(hk[:, sl])
        kf_ref[:, sl] = r
        kb_ref[:, sl] = r.astype(BF16)
    hv = _dot(x, w_ref[:, 2 * QK_WIDTH:QKV_COLS])
    vf_ref[...] = hv
    vb_ref[...] = hv.astype(BF16)


def _qkv_proj(x2d, w_qkv, tables, tm):
    n = x2d.shape[0]
    n_pos_blocks = tables[0].shape[0] // tm
    row = lambda i: (i, 0)
    tab = pl.BlockSpec((tm, LANES), lambda i: (i % n_pos_blocks, 0))
    wide = lambda: pl.BlockSpec((tm, QK_WIDTH), row)
    shp = lambda dt: jax.ShapeDtypeStruct((n, QK_WIDTH), dt)
    return pl.pallas_call(
        _qkv_kernel,
        grid=(n // tm,),
        in_specs=[pl.BlockSpec((tm, D_MODEL), row),
                  pl.BlockSpec((D_MODEL, QKV_COLS), lambda i: (0, 0)),
                  tab, tab, tab],
        out_specs=[wide() for _ in range(6)],
        out_shape=[shp(BF16), shp(BF16), shp(F32), shp(BF16), shp(F32), shp(BF16)],
        compiler_params=_params(("parallel",), 48),
        name="qkv_proj",
    )(x2d, w_qkv, *tables)


def _lambda(lp, lam_init):
    a = jnp.sum(lp[0:1] * lp[1:2], axis=-1, keepdims=True)
    b = jnp.sum(lp[2:3] * lp[3:4], axis=-1, keepdims=True)
    return jnp.exp(a) - jnp.exp(b) + lam_init


def _flash_kernel(lp_ref, q1_ref, q2_ref, k_ref, v_ref, o_ref, m_sc, l_sc, acc_sc, *, tq, lam_init):
    qi = pl.program_id(2)
    q = jnp.concatenate([q1_ref[...], q2_ref[...]], axis=0)
    m_sc[...] = jnp.full(m_sc.shape, -jnp.inf, F32)
    l_sc[...] = jnp.zeros(l_sc.shape, F32)
    acc_sc[...] = jnp.zeros(acc_sc.shape, F32)

    def block(ki, masked):
        start = pl.multiple_of(ki * tq, tq)
        k = k_ref[pl.ds(start, tq), :]
        v = v_ref[pl.ds(start, tq), :]
        s = _dot_nt(q, k)
        if masked:
            r = lax.broadcasted_iota(jnp.int32, s.shape, 0)
            r = jnp.where(r >= tq, r - tq, r)
            col = lax.broadcasted_iota(jnp.int32, s.shape, 1)
            s = jnp.where(col <= r, s, -jnp.inf)
        m_prev = m_sc[...]
        m_new = jnp.maximum(m_prev, jnp.max(s, axis=-1, keepdims=True))
        alpha = jnp.exp(m_prev - m_new)
        p = jnp.exp(s - m_new)
        l_sc[...] = alpha * l_sc[...] + jnp.sum(p, axis=-1, keepdims=True)
        acc_sc[...] = alpha * acc_sc[...] + _dot(p.astype(BF16), v)
        m_sc[...] = m_new

    def body(ki, carry):
        block(ki, False)
        return carry

    lax.fori_loop(0, qi, body, 0)
    block(qi, True)
    o = acc_sc[...] / l_sc[...]
    lam = _lambda(lp_ref[...], lam_init)
    o_ref[...] = o[:tq] - lam * o[tq:]


def _flash_attention(lp, q1, q2, kb, vb, lam_init, tq):
    b, s, _ = q1.shape
    qspec = pl.BlockSpec((None, tq, LANES), lambda bi, h, qi: (bi, qi, h))
    kspec = pl.BlockSpec((None, s, LANES), lambda bi, h, qi: (bi, 0, h))
    return pl.pallas_call(
        functools.partial(_flash_kernel, tq=tq, lam_init=lam_init),
        grid=(b, N_HEADS, s // tq),
        in_specs=[pl.BlockSpec((4, HEAD_DIM), lambda bi, h, qi: (0, 0)), qspec, qspec, kspec, kspec],
        out_specs=qspec,
        out_shape=jax.ShapeDtypeStruct((b, s, ATTN_WIDTH), F32),
        scratch_shapes=[pltpu.VMEM((2 * tq, 1), F32), pltpu.VMEM((2 * tq, 1), F32),
                        pltpu.VMEM((2 * tq, V_DIM), F32)],
        compiler_params=_params(("parallel", "parallel", "arbitrary"), 48),
        name="flash_diff_attn",
    )(lp, q1, q2, kb, vb)


def _paged_kernel(pt_ref, lp_ref, q1_ref, q2_ref, kn_ref, vn_ref, *rest, t_new, lam_init):
    np_ = PAGES_PER_STEP
    k_pages, v_pages = rest[:np_], rest[np_:2 * np_]
    o_ref, m_sc, l_sc, acc_sc, kc_sc, vc_sc = rest[2 * np_:]
    j = pl.program_id(1)
    rows = 2 * t_new * N_HEADS

    @pl.when(j == 0)
    def _():
        m_sc[...] = jnp.full(m_sc.shape, -jnp.inf, F32)
        l_sc[...] = jnp.zeros(l_sc.shape, F32)
        acc_sc[...] = jnp.zeros(acc_sc.shape, F32)

    head_of_lane = lax.broadcasted_iota(jnp.int32, (N_HEADS, QK_WIDTH), 1) // LANES
    own_head = head_of_lane == lax.broadcasted_iota(jnp.int32, (N_HEADS, QK_WIDTH), 0)
    pieces = []
    for q_ref in (q1_ref, q2_ref):
        for t in range(t_new):
            qt = jnp.broadcast_to(q_ref[t:t + 1, :], (N_HEADS, QK_WIDTH))
            pieces.append(jnp.where(own_head, qt, 0.0))
    qbd = jnp.concatenate(pieces, axis=0).astype(BF16)

    def update(s, v):
        m_prev = m_sc[...]
        m_new = jnp.maximum(m_prev, jnp.max(s, axis=-1, keepdims=True))
        alpha = jnp.exp(m_prev - m_new)
        p = jnp.exp(s - m_new)
        l_sc[...] = alpha * l_sc[...] + jnp.sum(p, axis=-1, keepdims=True)
        acc_sc[...] = alpha * acc_sc[...] + _dot(p.astype(BF16), v)
        m_sc[...] = m_new

    for i in range(np_):
        kc_sc[PAGE_SIZE * i:PAGE_SIZE * (i + 1), :] = k_pages[i][...].astype(BF16)
        vc_sc[PAGE_SIZE * i:PAGE_SIZE * (i + 1), :] = v_pages[i][...].astype(BF16)
    update(_dot_nt(qbd, kc_sc[...]), vc_sc[...])

    @pl.when(j == pl.num_programs(1) - 1)
    def _():
        s = _dot_nt(qbd, kn_ref[...].astype(BF16))
        r = lax.broadcasted_iota(jnp.int32, s.shape, 0)
        tok = (r // N_HEADS) % t_new
        col = lax.broadcasted_iota(jnp.int32, s.shape, 1)
        s = jnp.where(col <= tok, s, -jnp.inf)
        update(s, vn_ref[...].astype(BF16))
        o = acc_sc[...] / l_sc[...]
        lam = _lambda(lp_ref[...], lam_init)
        half = rows // 2
        d = o[:half] - lam * o[half:]
        for t in range(t_new):
            blk = jnp.where(own_head, d[N_HEADS * t:N_HEADS * (t + 1)], 0.0)
            o_ref[t:t + 1, :] = jnp.sum(blk, axis=0, keepdims=True)


def _paged_attention(lp, page_table, q1, q2, k_new, v_new, cache_k2, cache_v2, lam_init):
    bd, t_new, _ = q1.shape
    n_pages = page_table.shape[1]
    np_ = PAGES_PER_STEP
    rows = 2 * t_new * N_HEADS
    chunk = np_ * PAGE_SIZE
    seq = lambda b, j, pt: (b, 0, 0)
    qspec = pl.BlockSpec((None, t_new, QK_WIDTH), seq)
    nspec = pl.BlockSpec((None, PAGE_SIZE, QK_WIDTH), seq)

    def page_spec(i):
        return pl.BlockSpec((None, PAGE_SIZE, QK_WIDTH), lambda b, j, pt: (pt[b, j * np_ + i], 0, 0))

    grid_spec = pltpu.PrefetchScalarGridSpec(
        num_scalar_prefetch=1,
        grid=(bd, n_pages // np_),
        in_specs=([pl.BlockSpec((4, HEAD_DIM), lambda b, j, pt: (0, 0)), qspec, qspec, nspec, nspec]
                  + [page_spec(i) for i in range(np_)] + [page_spec(i) for i in range(np_)]),
        out_specs=qspec,
        scratch_shapes=[pltpu.VMEM((rows, 1), F32), pltpu.VMEM((rows, 1), F32),
                        pltpu.VMEM((rows, ATTN_WIDTH), F32),
                        pltpu.VMEM((chunk, QK_WIDTH), BF16), pltpu.VMEM((chunk, ATTN_WIDTH), BF16)],
    )
    return pl.pallas_call(
        functools.partial(_paged_kernel, t_new=t_new, lam_init=lam_init),
        grid_spec=grid_spec,
        out_shape=jax.ShapeDtypeStruct((bd, t_new, ATTN_WIDTH), F32),
        compiler_params=_params(("parallel", "arbitrary"), 48),
        name="paged_diff_attn",
    )(page_table, lp, q1, q2, k_new, v_new, *([cache_k2] * np_), *([cache_v2] * np_))


def _post_kernel(*refs, tm, tiles_per_seq, t_short, alpha, lam_init):
    long_seq = tiles_per_seq > 0
    (x_ref, o_ref, w2_ref, wa_ref, wc_ref, wo_ref, wr_ref, g_ref, cw_ref, cb_ref,
     l1g_ref, l1b_ref, br_ref) = refs[:13]
    if long_seq:
        st_ref, x1_ref, ti_ref, tg_ref, nc_ref, carry = refs[13:]
    else:
        h1_ref, h2_ref, x1_ref, ti_ref, tg_ref, u_ref = refs[13:]

    x = x_ref[...]
    h = _dot(x.astype(BF16), w2_ref[...])
    c0 = CONV_CH
    bg, cg, xc = h[:, 0:c0], h[:, c0:2 * c0], h[:, 2 * c0:3 * c0]
    ga = h[:, 3 * c0:3 * c0 + D_MODEL]
    gc = h[:, 3 * c0 + D_MODEL:]

    u = cg * xc
    row = lax.broadcasted_iota(jnp.int32, u.shape, 0)
    if long_seq:
        first = pl.program_id(0) % tiles_per_seq == 0
        st = st_ref[...]
        prev2 = jnp.where(first, st[0:1], carry[0:1])
        prev1 = jnp.where(first, st[1:2], carry[1:2])
        u1 = jnp.where(row == 0, prev1, pltpu.roll(u, 1, 0))
        u2 = jnp.where(row == 0, prev2, jnp.where(row == 1, prev1, pltpu.roll(u, 2, 0)))
        carry[0:2, :] = u[tm - 2:tm]
        nc_ref[...] = u[tm - 2:tm]
    else:
        tpos = row % t_short
        u1 = jnp.where(tpos >= 1, pltpu.roll(u, 1, 0), h1_ref[...])
        u2 = jnp.where(tpos >= 2, pltpu.roll(u, 2, 0), h2_ref[...])
        u_ref[...] = u
    cw = cw_ref[...]
    y_c = bg * (cb_ref[...] + u2 * cw[0:1] + u1 * cw[1:2] + u * cw[2:3])
    y_c = _dot(y_c.astype(BF16), wc_ref[...])

    o = o_ref[...]
    gain = g_ref[...]
    heads = []
    for hd in range(N_HEADS):
        oh = o[:, V_DIM * hd:V_DIM * (hd + 1)]
        ms = jnp.mean(oh * oh, axis=-1, keepdims=True)
        heads.append(oh * lax.rsqrt(ms + LN_EPS) * gain * (1.0 - lam_init))
    y_a = _dot(jnp.concatenate(heads, axis=1).astype(BF16), wa_ref[...])

    mix = jax.nn.sigmoid(ga) * y_a + jax.nn.sigmoid(gc) * y_c
    z = alpha * x + _dot(mix.astype(BF16), wo_ref[...])
    mu = jnp.mean(z, axis=-1, keepdims=True)
    zc = z - mu
    var = jnp.mean(zc * zc, axis=-1, keepdims=True)
    x1 = zc * lax.rsqrt(var + LN_EPS) * l1g_ref[...] + l1b_ref[...]
    x1_ref[...] = x1

    logits = _dot(x1.astype(BF16), wr_ref[...]) + br_ref[...]
    lane = lax.broadcasted_iota(jnp.int32, logits.shape, 1)
    logits = jnp.where(lane < N_EXPERTS, logits, -jnp.inf)
    idx_out = jnp.zeros(logits.shape, jnp.int32)
    val_out = jnp.zeros(logits.shape, F32)
    top0 = None
    denom = jnp.zeros((tm, 1), F32)
    for k in range(TOP_K):
        mx = jnp.max(logits, axis=-1, keepdims=True)
        idx = jnp.min(jnp.where(logits == mx, lane, LANES), axis=-1, keepdims=True)
        if k == 0:
            top0 = mx
        e = jnp.exp(mx - top0)
        denom = denom + e
        idx_out = jnp.where(lane == k, idx, idx_out)
        val_out = jnp.where(lane == k, e, val_out)
        logits = jnp.where(lane == idx, -jnp.inf, logits)
    ti_ref[...] = idx_out
    tg_ref[...] = val_out / denom


def _post_block(x2d, o2d, weights, vecs, *, tm, tiles_per_seq=0, conv_state=None, hist=None, t_short=0,
                alpha, lam_init):
    n = x2d.shape[0]
    row = lambda i: (i, 0)
    full = lambda a: pl.BlockSpec(a.shape, lambda i: (0,) * a.ndim)
    in_specs = [pl.BlockSpec((tm, D_MODEL), row), pl.BlockSpec((tm, ATTN_WIDTH), row)]
    in_specs += [full(w) for w in weights] + [full(v) for v in vecs]
    out_specs = [pl.BlockSpec((tm, D_MODEL), row), pl.BlockSpec((tm, LANES), row), pl.BlockSpec((tm, LANES), row)]
    out_shape = [jax.ShapeDtypeStruct((n, D_MODEL), F32), jax.ShapeDtypeStruct((n, LANES), jnp.int32),
                 jax.ShapeDtypeStruct((n, LANES), F32)]
    scratch = []
    if tiles_per_seq > 0:
        extra = [conv_state]
        in_specs.append(pl.BlockSpec((None, CONV_K - 1, CONV_CH), lambda i: (i // tiles_per_seq, 0, 0)))
        out_specs.append(pl.BlockSpec((None, CONV_K - 1, CONV_CH), lambda i: (i // tiles_per_seq, 0, 0)))
        out_shape.append(jax.ShapeDtypeStruct(conv_state.shape, F32))
        scratch.append(pltpu.VMEM((8, CONV_CH), F32))
    else:
        extra = list(hist)
        in_specs += [pl.BlockSpec((tm, CONV_CH), row)] * 2
        out_specs.append(pl.BlockSpec((tm, CONV_CH), row))
        out_shape.append(jax.ShapeDtypeStruct((n, CONV_CH), F32))
    return pl.pallas_call(
        functools.partial(_post_kernel, tm=tm, tiles_per_seq=tiles_per_seq, t_short=t_short, alpha=alpha,
                          lam_init=lam_init),
        grid=(n // tm,),
        in_specs=in_specs,
        out_specs=out_specs,
        out_shape=out_shape,
        scratch_shapes=scratch,
        compiler_params=_params(("arbitrary",), 56),
        name="post_block",
    )(x2d, o2d, *weights, *vecs, *extra)


def _route(top_i, gate, tm):
    n = top_i.shape[0]
    a = n * TOP_K
    flat_e = top_i.reshape(a)
    onehot = (flat_e[:, None] == jnp.arange(N_EXPERTS, dtype=jnp.int32)[None, :]).astype(jnp.int32)
    csum = jnp.cumsum(onehot, axis=0)
    rank = jnp.take_along_axis(csum, flat_e[:, None], axis=1)[:, 0] - 1
    counts = csum[-1]
    padded = (counts + tm - 1) // tm * tm
    pad_end = jnp.cumsum(padded)
    pad_start = pad_end - padded
    slot = (pad_start[flat_e] + rank).astype(jnp.int32)
    n_tiles = (a + N_EXPERTS * (tm - 1) + tm - 1) // tm
    n_slots = n_tiles * tm
    tok = jnp.arange(a, dtype=jnp.int32) // TOP_K
    slot_tok = jnp.zeros((n_slots,), jnp.int32).at[slot].set(tok)
    slot_gate = jnp.zeros((n_slots,), F32).at[slot].set(gate.reshape(a))
    tile_e = jnp.minimum(jnp.searchsorted(pad_end, jnp.arange(n_tiles, dtype=jnp.int32) * tm, side='right'),
                         N_EXPERTS - 1).astype(jnp.int32)
    n_used = (pad_end[-1] // tm).astype(jnp.int32).reshape(1)
    return slot, slot_tok, slot_gate, tile_e, n_used


def _row_copy(src_hbm, src_row, dst, dst_row, sem):
    return pltpu.make_async_copy(src_hbm.at[pl.ds(src_row, 1), :], dst.at[pl.ds(dst_row, 1), :], sem)


def _gather_kernel(idx_ref, x_hbm, o_ref, buf, sem, *, rows):
    def issue(r, c):
        _row_copy(x_hbm, idx_ref[r], buf, r, sem).start()
        return c

    lax.fori_loop(0, rows, issue, 0)

    def wait(r, c):
        _row_copy(x_hbm, 0, buf, r, sem).wait()
        return c

    lax.fori_loop(0, rows, wait, 0)
    o_ref[...] = buf[...].astype(BF16)


def _gather_rows(x1, slot_tok, rows):
    n_slots = slot_tok.shape[0]
    return pl.pallas_call(
        functools.partial(_gather_kernel, rows=rows),
        grid=(n_slots // rows,),
        in_specs=[pl.BlockSpec((rows,), lambda i: (i,), memory_space=pltpu.SMEM),
                  pl.BlockSpec(memory_space=pl.ANY)],
        out_specs=pl.BlockSpec((rows, D_MODEL), lambda i: (i, 0)),
        out_shape=jax.ShapeDtypeStruct((n_slots, D_MODEL), BF16),
        scratch_shapes=[pltpu.VMEM((rows, D_MODEL), F32), pltpu.SemaphoreType.DMA(())],
        compiler_params=_params(("arbitrary",), 32),
        name="moe_gather",
    )(slot_tok, x1)


def _ffn_kernel(te_ref, nu_ref, xs_ref, wg_ref, wu_ref, bg_ref, bu_ref, wd_ref, bd_ref, sg_ref, o_ref):
    i = pl.program_id(0)

    @pl.when(i < nu_ref[0])
    def _():
        x = xs_ref[...]
        g = jnp.minimum(_dot(x, wg_ref[...]) + bg_ref[...], SWIGLU_LIMIT)
        u = jnp.clip(_dot(x, wu_ref[...]) + bu_ref[...], -SWIGLU_LIMIT, SWIGLU_LIMIT)
        act = (u + 1.0) * g * jax.nn.sigmoid(SWIGLU_ALPHA * g)
        y = _dot(act.astype(BF16), wd_ref[...]) + bd_ref[...]
        o_ref[...] = y * sg_ref[...]

    @pl.when(i >= nu_ref[0])
    def _():
        o_ref[...] = jnp.zeros(o_ref.shape, F32)


def _expert_ffn(xs, tile_e, n_used, wg, wu, bg, bu, wd, bd, slot_gate, tm):
    n_slots = xs.shape[0]
    row = lambda i, te, nu: (i, 0)
    wsel = lambda i, te, nu: (te[i], 0, 0)
    grid_spec = pltpu.PrefetchScalarGridSpec(
        num_scalar_prefetch=2,
        grid=(n_slots // tm,),
        in_specs=[pl.BlockSpec((tm, D_MODEL), row),
                  pl.BlockSpec((None, D_MODEL, D_FF), wsel), pl.BlockSpec((None, D_MODEL, D_FF), wsel),
                  pl.BlockSpec((None, 1, D_FF), wsel), pl.BlockSpec((None, 1, D_FF), wsel),
                  pl.BlockSpec((None, D_FF, D_MODEL), wsel), pl.BlockSpec((None, 1, D_MODEL), wsel),
                  pl.BlockSpec((tm, 1), row)],
        out_specs=pl.BlockSpec((tm, D_MODEL), row),
    )
    return pl.pallas_call(
        _ffn_kernel,
        grid_spec=grid_spec,
        out_shape=jax.ShapeDtypeStruct((n_slots, D_MODEL), F32),
        compiler_params=_params(("arbitrary",), 48),
        name="moe_ffn",
    )(tile_e, n_used, xs, wg, wu, bg, bu, wd, bd, slot_gate.reshape(n_slots, 1))


def _combine_kernel(slot_ref, x1_ref, y_hbm, g_ref, b_ref, o_ref, buf, sem, *, rows, alpha):
    def issue(r, c):
        for k in range(TOP_K):
            _row_copy(y_hbm, slot_ref[r * TOP_K + k], buf.at[k], r, sem).start()
        return c

    lax.fori_loop(0, rows, issue, 0)

    def wait(r, c):
        for k in range(TOP_K):
            _row_copy(y_hbm, 0, buf.at[k], r, sem).wait()
        return c

    lax.fori_loop(0, rows, wait, 0)
    f = (buf[0] + buf[1]) + (buf[2] + buf[3])
    z = alpha * x1_ref[...] + f
    mu = jnp.mean(z, axis=-1, keepdims=True)
    zc = z - mu
    var = jnp.mean(zc * zc, axis=-1, keepdims=True)
    o_ref[...] = zc * lax.rsqrt(var + LN_EPS) * g_ref[...] + b_ref[...]


def _combine(x1, slot, yb, ln_g, ln_b, rows, alpha):
    n = x1.shape[0]
    row = lambda i: (i, 0)
    vec = pl.BlockSpec((1, D_MODEL), lambda i: (0, 0))
    return pl.pallas_call(
        functools.partial(_combine_kernel, rows=rows, alpha=alpha),
        grid=(n // rows,),
        in_specs=[pl.BlockSpec((rows * TOP_K,), lambda i: (i,), memory_space=pltpu.SMEM),
                  pl.BlockSpec((rows, D_MODEL), row),
                  pl.BlockSpec(memory_space=pl.ANY), vec, vec],
        out_specs=pl.BlockSpec((rows, D_MODEL), row),
        out_shape=jax.ShapeDtypeStruct((n, D_MODEL), F32),
        scratch_shapes=[pltpu.VMEM((TOP_K, rows, D_MODEL), F32), pltpu.SemaphoreType.DMA(())],
        compiler_params=_params(("arbitrary",), 32),
        name="moe_combine",
    )(slot, x1, yb, ln_g, ln_b)


def kernel(x_prompt, x_sample, cache_k, cache_v, state_conv, page_table, w_in, lambda_q1, lambda_k1,
           lambda_q2, lambda_k2, subln_g, w_attn_out, w_conv, b_conv, w_conv_out, w_o, ln1_g, ln1_b,
           w_router, b_router, w_gate_up, b_gate_up, w_down, b_down, ln2_g, ln2_b):
    bp, s, d = x_prompt.shape
    bd, t_new, _ = x_sample.shape
    depth = w_in.shape[0]
    n_phys = cache_k.shape[1]
    past = page_table.shape[1] * PAGE_SIZE
    alpha = (2.0 * depth) ** 0.25
    np_rows, ns_rows = bp * s, bd * t_new
    tm_p = 512
    tq = 512

    tab_p = _rope_tables(jnp.arange(s))
    tab_s = _rope_tables(jnp.tile(past + jnp.arange(t_new), bd))

    hp = x_prompt.reshape(np_rows, d)
    hs = x_sample.reshape(ns_rows, d)
    outs = {k: [] for k in ("kp", "vp", "cp", "ks", "vs", "cs")}
    for l in range(depth):
        lam_init = 0.8 - 0.6 * math.exp(-0.3 * l)
        lp = jnp.concatenate([lambda_q1[l:l + 1], lambda_k1[l:l + 1], lambda_q2[l:l + 1], lambda_k2[l:l + 1]], axis=0)
        w_l = w_in[l]
        w_qkv = w_l[:, :QKV_COLS].astype(BF16)
        w_rest = w_l[:, QKV_COLS:].astype(BF16)
        w_r = jnp.pad(w_router[l], ((0, 0), (0, LANES - N_EXPERTS))).astype(BF16)
        b_r = jnp.pad(b_router[l], (0, LANES - N_EXPERTS)).reshape(1, LANES)
        weights = (w_rest, w_attn_out[l].astype(BF16), w_conv_out[l].astype(BF16), w_o[l].astype(BF16), w_r)
        vecs = (subln_g[l].reshape(1, V_DIM), w_conv[l],
                b_conv[l].reshape(1, CONV_CH), ln1_g[l].reshape(1, d), ln1_b[l].reshape(1, d), b_r)

        q1, q2, kf, kb, vf, vb = _qkv_proj(hp, w_qkv, tab_p, tm_p)
        shp = (bp, s, QK_WIDTH)
        o_p = _flash_attention(lp, q1.reshape(shp), q2.reshape(shp), kb.reshape(shp), vb.reshape(shp), lam_init, tq)
        x1_p, ti_p, tg_p, nc_p = _post_block(
            hp, o_p.reshape(np_rows, ATTN_WIDTH), weights, vecs, tm=tm_p // 2, tiles_per_seq=s // (tm_p // 2),
            conv_state=jnp.zeros((bp, CONV_K - 1, CONV_CH), F32), alpha=alpha, lam_init=lam_init)
        outs["kp"].append(kf.reshape(bp, s, N_HEADS, 2, HEAD_DIM))
        outs["vp"].append(vf.reshape(bp, s, N_HEADS, V_DIM))
        outs["cp"].append(nc_p)

        q1s, q2s, kfs, _, vfs, _ = _qkv_proj(hs, w_qkv, tab_s, ns_rows)
        seq3 = lambda a: a.reshape(bd, t_new, QK_WIDTH)
        pad_page = lambda a: jnp.pad(seq3(a), ((0, 0), (0, PAGE_SIZE - t_new), (0, 0)))
        o_s = _paged_attention(
            lp, page_table, seq3(q1s).astype(F32), seq3(q2s).astype(F32), pad_page(kfs), pad_page(vfs),
            cache_k[l].reshape(n_phys, PAGE_SIZE, QK_WIDTH), cache_v[l].reshape(n_phys, PAGE_SIZE, ATTN_WIDTH),
            lam_init)
        st = state_conv[l]
        zero = jnp.zeros((bd, 1, CONV_CH), F32)
        hist1 = jnp.concatenate([st[:, 1:2]] + [zero] * (t_new - 1), axis=1).reshape(ns_rows, CONV_CH)
        hist2 = jnp.concatenate([st[:, 0:1], st[:, 1:2]] + [zero] * (t_new - 2), axis=1).reshape(ns_rows, CONV_CH)
        x1_s, ti_s, tg_s, u_s = _post_block(
            hs, o_s.reshape(ns_rows, ATTN_WIDTH), weights, vecs, tm=ns_rows, hist=(hist1, hist2), t_short=t_new,
            alpha=alpha, lam_init=lam_init)
        outs["ks"].append(kfs.reshape(bd, t_new, N_HEADS, 2, HEAD_DIM))
        outs["vs"].append(vfs.reshape(bd, t_new, N_HEADS, V_DIM))
        outs["cs"].append(u_s.reshape(bd, t_new, CONV_CH)[:, t_new - (CONV_K - 1):])

        x1 = jnp.concatenate([x1_p, x1_s], axis=0)
        top_i = jnp.concatenate([ti_p[:, :TOP_K], ti_s[:, :TOP_K]], axis=0)
        gate = jnp.concatenate([tg_p[:, :TOP_K], tg_s[:, :TOP_K]], axis=0)
        slot, slot_tok, slot_gate, tile_e, n_used = _route(top_i, gate, FFN_TILE)
        xs = _gather_rows(x1, slot_tok, 256)
        wgu = w_gate_up[l]
        bgu = b_gate_up[l]
        yb = _expert_ffn(
            xs, tile_e, n_used, wgu[:, :, 0::2].astype(BF16), wgu[:, :, 1::2].astype(BF16),
            bgu[:, None, 0::2], bgu[:, None, 1::2], w_down[l].astype(BF16), b_down[l][:, None, :],
            slot_gate, FFN_TILE)
        g2, b2 = ln2_g[l].reshape(1, d), ln2_b[l].reshape(1, d)
        hp = _combine(x1_p, slot[:np_rows * TOP_K], yb, g2, b2, 256, alpha)
        hs = _combine(x1_s, slot[np_rows * TOP_K:], yb, g2, b2, ns_rows, alpha)

    return (hp.reshape(bp, s, d), hs.reshape(bd, t_new, d),
            jnp.stack(outs["kp"]), jnp.stack(outs["vp"]), jnp.stack(outs["cp"]),
            jnp.stack(outs["ks"]), jnp.stack(outs["vs"]), jnp.stack(outs["cs"]))
```

```python
import functools
import math

import jax
import jax.numpy as jnp
from jax import lax
from jax.experimental import pallas as pl
from jax.experimental.pallas import tpu as pltpu

F32 = jnp.float32
BF16 = jnp.bfloat16

D_MODEL = 1024
N_HEADS = 8
HEAD_DIM = 64
V_DIM = 2 * HEAD_DIM
QK_WIDTH = N_HEADS * 2 * HEAD_DIM
ATTN_WIDTH = N_HEADS * V_DIM
ROT_DIM = HEAD_DIM // 4
ROPE_THETA = 500000.0
CONV_CH = D_MODEL // 2
CONV_K = 3
N_EXPERTS = 32
TOP_K = 4
D_FF = D_MODEL
SWIGLU_LIMIT = 7.0
SWIGLU_ALPHA = 1.702
LN_EPS = 1e-5
PAGE_SIZE = 128
LANES = 128
MIB = 1024 * 1024

QKV_COLS = 2 * QK_WIDTH + ATTN_WIDTH
REST_COLS = 3 * CONV_CH + 2 * D_MODEL

PAGES_PER_STEP = 8
FFN_TILE = 512


def _params(semantics, vmem_mib):
    return pltpu.CompilerParams(dimension_semantics=semantics, vmem_limit_bytes=vmem_mib * MIB)


def _dot(a, b):
    return jnp.dot(a, b, preferred_element_type=F32)


def _dot_nt(a, b):
    return lax.dot_general(a, b, (((1,), (1,)), ((), ())), preferred_element_type=F32)


def _rope_tables(pos):
    inv = ROPE_THETA ** (-jnp.arange(0, ROT_DIM, 2, dtype=F32) / ROT_DIM)
    ang = pos.astype(F32)[:, None] * inv[None, :]
    cos, sin = jnp.cos(ang), jnp.sin(ang)
    n = pos.shape[0]
    half = ROT_DIM // 2
    rest = HEAD_DIM - ROT_DIM
    c = jnp.concatenate([cos, cos, jnp.ones((n, rest), F32)], axis=1)
    s_up = jnp.concatenate([-sin, jnp.zeros((n, half + rest), F32)], axis=1)
    s_dn = jnp.concatenate([jnp.zeros((n, half), F32), sin, jnp.zeros((n, rest), F32)], axis=1)
    rep = LANES // HEAD_DIM
    return tuple(jnp.tile(t, (1, rep)) for t in (c, s_up, s_dn))


def _qkv_kernel(x_ref, w_ref, c_ref, su_ref, sd_ref, q1_ref, q2_ref, kf_ref, kb_ref, vf_ref, vt_ref):
    x = x_ref[...].astype(BF16)
    c, s_up, s_dn = c_ref[...], su_ref[...], sd_ref[...]
    half = ROT_DIM // 2
    first_map = lax.broadcasted_iota(jnp.int32, c.shape, 1) < HEAD_DIM
    scale = HEAD_DIM ** -0.5

    def rope(blk):
        return blk * c + pltpu.roll(blk, LANES - half, 1) * s_up + pltpu.roll(blk, half, 1) * s_dn

    hq = _dot(x, w_ref[:, 0:QK_WIDTH])
    for j in range(QK_WIDTH // LANES):
        sl = slice(LANES * j, LANES * (j + 1))
        r = rope(hq[:, sl]) * scale
        q1_ref[:, sl] = jnp.where(first_map, r, 0.0).astype(BF16)
        q2_ref[:, sl] = jnp.where(first_map, 0.0, r).astype(BF16)
    hk = _dot(x, w_ref[:, QK_WIDTH:2 * QK_WIDTH])
    for j in range(QK_WIDTH // LANES):
        sl = slice(LANES * j, LANES * (j + 1))
        r = rope(hk[:, sl])
        kf_ref[:, sl] = r
        kb_ref[:, sl] = r.astype(BF16)
    hv = _dot(x, w_ref[:, 2 * QK_WIDTH:QKV_COLS])
    vf_ref[...] = hv
    vt_ref[...] = hv.T.astype(BF16)


def _qkv_proj(x2d, w_qkv, tables, tm):
    n = x2d.shape[0]
    seq = tables[0].shape[0]
    n_pos_blocks = seq // tm
    row = lambda i: (i, 0)
    tab = pl.BlockSpec((tm, LANES), lambda i: (i % n_pos_blocks, 0))
    wide = lambda: pl.BlockSpec((tm, QK_WIDTH), row)
    shp = lambda dt: jax.ShapeDtypeStruct((n, QK_WIDTH), dt)
    vt_spec = pl.BlockSpec((None, ATTN_WIDTH, tm), lambda i: (i // n_pos_blocks, 0, i % n_pos_blocks))
    return pl.pallas_call(
        _qkv_kernel,
        grid=(n // tm,),
        in_specs=[pl.BlockSpec((tm, D_MODEL), row),
                  pl.BlockSpec((D_MODEL, QKV_COLS), lambda i: (0, 0)),
                  tab, tab, tab],
        out_specs=[wide() for _ in range(5)] + [vt_spec],
        out_shape=[shp(BF16), shp(BF16), shp(F32), shp(BF16), shp(F32),
                   jax.ShapeDtypeStruct((n // seq, ATTN_WIDTH, seq), BF16)],
        compiler_params=_params(("parallel",), 48),
        name="qkv_proj",
    )(x2d, w_qkv, *tables)


def _lambda(lp, lam_init):
    a = jnp.sum(lp[0:1] * lp[1:2], axis=-1, keepdims=True)
    b = jnp.sum(lp[2:3] * lp[3:4], axis=-1, keepdims=True)
    return jnp.exp(a) - jnp.exp(b) + lam_init


FLASH_GROUP = 256


def _flash_kernel(lp_ref, q1_ref, q2_ref, k_ref, vt_ref, o_ref, m_sc, l_sc, acc_sc, *, tq, lam_init):
    qi = pl.program_id(2)
    m_sc[...] = jnp.full(m_sc.shape, -jnp.inf, F32)
    l_sc[...] = jnp.zeros(l_sc.shape, F32)
    acc_sc[...] = jnp.zeros(acc_sc.shape, F32)
    gw = FLASH_GROUP
    groups = [(m, c0) for m in range(2) for c0 in range(0, tq, gw)]

    def block(ki, masked):
        start = pl.multiple_of(ki * tq, tq)
        k = k_ref[pl.ds(start, tq), :]
        vt = vt_ref[:, pl.ds(start, tq)]
        scores = [_dot_nt(k, q_ref[...]) for q_ref in (q1_ref, q2_ref)]
        probs, alphas = [], []
        for m, c0 in groups:
            cols = slice(m * tq + c0, m * tq + c0 + gw)
            s = scores[m][:, c0:c0 + gw]
            if masked:
                key = lax.broadcasted_iota(jnp.int32, s.shape, 0)
                qry = lax.broadcasted_iota(jnp.int32, s.shape, 1) + c0
                s = jnp.where(key <= qry, s, -jnp.inf)
            m_prev = m_sc[:, cols]
            m_new = jnp.maximum(m_prev, jnp.max(s, axis=0, keepdims=True))
            alpha = jnp.exp(m_prev - m_new)
            p = jnp.exp(s - m_new)
            l_sc[:, cols] = alpha * l_sc[:, cols] + jnp.sum(p, axis=0, keepdims=True)
            m_sc[:, cols] = m_new
            probs.append(p.astype(BF16))
            alphas.append(alpha)
        for m in range(2):
            cols = slice(m * tq, (m + 1) * tq)
            per_map = tq // gw
            p = jnp.concatenate(probs[m * per_map:(m + 1) * per_map], axis=1)
            alpha = jnp.concatenate(alphas[m * per_map:(m + 1) * per_map], axis=1)
            acc_sc[:, cols] = alpha * acc_sc[:, cols] + _dot(vt, p)

    def body(ki, carry):
        block(ki, False)
        return carry

    lax.fori_loop(0, qi, body, 0)
    block(qi, True)
    o = acc_sc[...] / l_sc[...]
    lam = _lambda(lp_ref[...], lam_init)
    o_ref[...] = (o[:, :tq] - lam * o[:, tq:]).T


def _flash_attention(lp, q1, q2, kb, vt, lam_init, tq):
    b, s, _ = q1.shape
    qspec = pl.BlockSpec((None, tq, LANES), lambda bi, h, qi: (bi, qi, h))
    kspec = pl.BlockSpec((None, s, LANES), lambda bi, h, qi: (bi, 0, h))
    vspec = pl.BlockSpec((None, V_DIM, s), lambda bi, h, qi: (bi, h, 0))
    return pl.pallas_call(
        functools.partial(_flash_kernel, tq=tq, lam_init=lam_init),
        grid=(b, N_HEADS, s // tq),
        in_specs=[pl.BlockSpec((4, HEAD_DIM), lambda bi, h, qi: (0, 0)), qspec, qspec, kspec, vspec],
        out_specs=qspec,
        out_shape=jax.ShapeDtypeStruct((b, s, ATTN_WIDTH), F32),
        scratch_shapes=[pltpu.VMEM((1, 2 * tq), F32), pltpu.VMEM((1, 2 * tq), F32),
                        pltpu.VMEM((V_DIM, 2 * tq), F32)],
        compiler_params=_params(("parallel", "parallel", "arbitrary"), 48),
        name="flash_diff_attn",
    )(lp, q1, q2, kb, vt)


def _paged_kernel(pt_ref, lp_ref, q1_ref, q2_ref, kn_ref, vn_ref, *rest, t_new, lam_init):
    np_ = PAGES_PER_STEP
    k_pages, v_pages = rest[:np_], rest[np_:2 * np_]
    o_ref, m_sc, l_sc, acc_sc, kc_sc, vc_sc = rest[2 * np_:]
    j = pl.program_id(1)
    rows = 2 * t_new * N_HEADS

    @pl.when(j == 0)
    def _():
        m_sc[...] = jnp.full(m_sc.shape, -jnp.inf, F32)
        l_sc[...] = jnp.zeros(l_sc.shape, F32)
        acc_sc[...] = jnp.zeros(acc_sc.shape, F32)

    head_of_lane = lax.broadcasted_iota(jnp.int32, (N_HEADS, QK_WIDTH), 1) // LANES
    own_head = head_of_lane == lax.broadcasted_iota(jnp.int32, (N_HEADS, QK_WIDTH), 0)
    pieces = []
    for q_ref in (q1_ref, q2_ref):
        for t in range(t_new):
            qt = jnp.broadcast_to(q_ref[t:t + 1, :], (N_HEADS, QK_WIDTH))
            pieces.append(jnp.where(own_head, qt, 0.0))
    qbd = jnp.concatenate(pieces, axis=0).astype(BF16)

    def update(s, v):
        m_prev = m_sc[...]
        m_new = jnp.maximum(m_prev, jnp.max(s, axis=-1, keepdims=True))
        alpha = jnp.exp(m_prev - m_new)
        p = jnp.exp(s - m_new)
        l_sc[...] = alpha * l_sc[...] + jnp.sum(p, axis=-1, keepdims=True)
        acc_sc[...] = alpha * acc_sc[...] + _dot(p.astype(BF16), v)
        m_sc[...] = m_new

    for i in range(np_):
        kc_sc[PAGE_SIZE * i:PAGE_SIZE * (i + 1), :] = k_pages[i][...]
        vc_sc[PAGE_SIZE * i:PAGE_SIZE * (i + 1), :] = v_pages[i][...]
    update(_dot_nt(qbd, kc_sc[...]), vc_sc[...])

    @pl.when(j == pl.num_programs(1) - 1)
    def _():
        s = _dot_nt(qbd, kn_ref[...].astype(BF16))
        r = lax.broadcasted_iota(jnp.int32, s.shape, 0)
        tok = (r // N_HEADS) % t_new
        col = lax.broadcasted_iota(jnp.int32, s.shape, 1)
        s = jnp.where(col <= tok, s, -jnp.inf)
        update(s, vn_ref[...].astype(BF16))
        o = acc_sc[...] / l_sc[...]
        lam = _lambda(lp_ref[...], lam_init)
        half = rows // 2
        d = o[:half] - lam * o[half:]
        for t in range(t_new):
            blk = jnp.where(own_head, d[N_HEADS * t:N_HEADS * (t + 1)], 0.0)
            o_ref[t:t + 1, :] = jnp.sum(blk, axis=0, keepdims=True)


def _paged_attention(lp, page_table, q1, q2, k_new, v_new, cache_k2, cache_v2, lam_init):
    bd, t_new, _ = q1.shape
    n_pages = page_table.shape[1]
    np_ = PAGES_PER_STEP
    rows = 2 * t_new * N_HEADS
    chunk = np_ * PAGE_SIZE
    seq = lambda b, j, pt: (b, 0, 0)
    qspec = pl.BlockSpec((None, t_new, QK_WIDTH), seq)
    nspec = pl.BlockSpec((None, PAGE_SIZE, QK_WIDTH), seq)

    def page_spec(i):
        return pl.BlockSpec((None, PAGE_SIZE, QK_WIDTH), lambda b, j, pt: (pt[b, j * np_ + i], 0, 0))

    grid_spec = pltpu.PrefetchScalarGridSpec(
        num_scalar_prefetch=1,
        grid=(bd, n_pages // np_),
        in_specs=([pl.BlockSpec((4, HEAD_DIM), lambda b, j, pt: (0, 0)), qspec, qspec, nspec, nspec]
                  + [page_spec(i) for i in range(np_)] + [page_spec(i) for i in range(np_)]),
        out_specs=qspec,
        scratch_shapes=[pltpu.VMEM((rows, 1), F32), pltpu.VMEM((rows, 1), F32),
                        pltpu.VMEM((rows, ATTN_WIDTH), F32),
                        pltpu.VMEM((chunk, QK_WIDTH), BF16), pltpu.VMEM((chunk, ATTN_WIDTH), BF16)],
    )
    return pl.pallas_call(
        functools.partial(_paged_kernel, t_new=t_new, lam_init=lam_init),
        grid_spec=grid_spec,
        out_shape=jax.ShapeDtypeStruct((bd, t_new, ATTN_WIDTH), F32),
        compiler_params=_params(("parallel", "arbitrary"), 48),
        name="paged_diff_attn",
    )(page_table, lp, q1, q2, k_new, v_new, *([cache_k2] * np_), *([cache_v2] * np_))


def _post_kernel(*refs, tm, tiles_per_seq, t_short, alpha, lam_init):
    long_seq = tiles_per_seq > 0
    (x_ref, o_ref, w2_ref, wa_ref, wc_ref, wo_ref, wr_ref, g_ref, cw_ref, cb_ref,
     l1g_ref, l1b_ref, br_ref) = refs[:13]
    if long_seq:
        st_ref, x1_ref, ti_ref, tg_ref, nc_ref, carry = refs[13:]
    else:
        h1_ref, h2_ref, x1_ref, ti_ref, tg_ref, u_ref = refs[13:]

    x = x_ref[...]
    h = _dot(x.astype(BF16), w2_ref[...])
    c0 = CONV_CH
    bg, cg, xc = h[:, 0:c0], h[:, c0:2 * c0], h[:, 2 * c0:3 * c0]
    ga = h[:, 3 * c0:3 * c0 + D_MODEL]
    gc = h[:, 3 * c0 + D_MODEL:]

    u = cg * xc
    row = lax.broadcasted_iota(jnp.int32, u.shape, 0)
    if long_seq:
        first = pl.program_id(0) % tiles_per_seq == 0
        st = st_ref[...]
        prev2 = jnp.where(first, st[0:1], carry[0:1])
        prev1 = jnp.where(first, st[1:2], carry[1:2])
        u1 = jnp.where(row == 0, prev1, pltpu.roll(u, 1, 0))
        u2 = jnp.where(row == 0, prev2, jnp.where(row == 1, prev1, pltpu.roll(u, 2, 0)))
        carry[0:2, :] = u[tm - 2:tm]
        nc_ref[...] = u[tm - 2:tm]
    else:
        tpos = row % t_short
        u1 = jnp.where(tpos >= 1, pltpu.roll(u, 1, 0), h1_ref[...])
        u2 = jnp.where(tpos >= 2, pltpu.roll(u, 2, 0), h2_ref[...])
        u_ref[...] = u
    cw = cw_ref[...]
    y_c = bg * (cb_ref[...] + u2 * cw[0:1] + u1 * cw[1:2] + u * cw[2:3])
    y_c = _dot(y_c.astype(BF16), wc_ref[...])

    o = o_ref[...]
    gain = g_ref[...]
    heads = []
    for hd in range(N_HEADS):
        oh = o[:, V_DIM * hd:V_DIM * (hd + 1)]
        ms = jnp.mean(oh * oh, axis=-1, keepdims=True)
        heads.append(oh * lax.rsqrt(ms + LN_EPS) * gain * (1.0 - lam_init))
    y_a = _dot(jnp.concatenate(heads, axis=1).astype(BF16), wa_ref[...])

    mix = jax.nn.sigmoid(ga) * y_a + jax.nn.sigmoid(gc) * y_c
    z = alpha * x + _dot(mix.astype(BF16), wo_ref[...])
    mu = jnp.mean(z, axis=-1, keepdims=True)
    zc = z - mu
    var = jnp.mean(zc * zc, axis=-1, keepdims=True)
    x1 = zc * lax.rsqrt(var + LN_EPS) * l1g_ref[...] + l1b_ref[...]
    x1_ref[...] = x1

    logits = _dot(x1.astype(BF16), wr_ref[...]) + br_ref[...]
    lane = lax.broadcasted_iota(jnp.int32, logits.shape, 1)
    logits = jnp.where(lane < N_EXPERTS, logits, -jnp.inf)
    idx_out = jnp.zeros(logits.shape, jnp.int32)
    val_out = jnp.zeros(logits.shape, F32)
    top0 = None
    denom = jnp.zeros((tm, 1), F32)
    for k in range(TOP_K):
        mx = jnp.max(logits, axis=-1, keepdims=True)
        idx = jnp.min(jnp.where(logits == mx, lane, LANES), axis=-1, keepdims=True)
        if k == 0:
            top0 = mx
        e = jnp.exp(mx - top0)
        denom = denom + e
        idx_out = jnp.where(lane == k, idx, idx_out)
        val_out = jnp.where(lane == k, e, val_out)
        logits = jnp.where(lane == idx, -jnp.inf, logits)
    ti_ref[...] = idx_out
    tg_ref[...] = val_out / denom


def _post_block(x2d, o2d, weights, vecs, *, tm, tiles_per_seq=0, conv_state=None, hist=None, t_short=0,
                alpha, lam_init):
    n = x2d.shape[0]
    row = lambda i: (i, 0)
    full = lambda a: pl.BlockSpec(a.shape, lambda i: (0,) * a.ndim)
    in_specs = [pl.BlockSpec((tm, D_MODEL), row), pl.BlockSpec((tm, ATTN_WIDTH), row)]
    in_specs += [full(w) for w in weights] + [full(v) for v in vecs]
    out_specs = [pl.BlockSpec((tm, D_MODEL), row), pl.BlockSpec((tm, LANES), row), pl.BlockSpec((tm, LANES), row)]
    out_shape = [jax.ShapeDtypeStruct((n, D_MODEL), F32), jax.ShapeDtypeStruct((n, LANES), jnp.int32),
                 jax.ShapeDtypeStruct((n, LANES), F32)]
    scratch = []
    if tiles_per_seq > 0:
        extra = [conv_state]
        in_specs.append(pl.BlockSpec((None, CONV_K - 1, CONV_CH), lambda i: (i // tiles_per_seq, 0, 0)))
        out_specs.append(pl.BlockSpec((None, CONV_K - 1, CONV_CH), lambda i: (i // tiles_per_seq, 0, 0)))
        out_shape.append(jax.ShapeDtypeStruct(conv_state.shape, F32))
        scratch.append(pltpu.VMEM((8, CONV_CH), F32))
    else:
        extra = list(hist)
        in_specs += [pl.BlockSpec((tm, CONV_CH), row)] * 2
        out_specs.append(pl.BlockSpec((tm, CONV_CH), row))
        out_shape.append(jax.ShapeDtypeStruct((n, CONV_CH), F32))
    return pl.pallas_call(
        functools.partial(_post_kernel, tm=tm, tiles_per_seq=tiles_per_seq, t_short=t_short, alpha=alpha,
                          lam_init=lam_init),
        grid=(n // tm,),
        in_specs=in_specs,
        out_specs=out_specs,
        out_shape=out_shape,
        scratch_shapes=scratch,
        compiler_params=_params(("arbitrary",), 56),
        name="post_block",
    )(x2d, o2d, *weights, *vecs, *extra)


def _route(top_i, tm):
    n = top_i.shape[0]
    a = n * TOP_K
    flat_e = top_i.reshape(a)
    onehot = (flat_e[:, None] == jnp.arange(N_EXPERTS, dtype=jnp.int32)[None, :]).astype(jnp.int32)
    csum = jnp.cumsum(onehot, axis=0)
    rank = jnp.take_along_axis(csum, flat_e[:, None], axis=1)[:, 0] - 1
    counts = csum[-1]
    padded = (counts + tm - 1) // tm * tm
    pad_end = jnp.cumsum(padded)
    pad_start = pad_end - padded
    slot = (pad_start[flat_e] + rank).astype(jnp.int32)
    n_tiles = (a + N_EXPERTS * (tm - 1) + tm - 1) // tm
    tile_start = jnp.arange(n_tiles, dtype=jnp.int32) * tm
    tile_e = jnp.minimum(jnp.sum((pad_end[None, :] <= tile_start[:, None]).astype(jnp.int32), axis=1),
                         N_EXPERTS - 1).astype(jnp.int32)
    n_used = (pad_end[-1] // tm).astype(jnp.int32).reshape(1)
    return slot, tile_e, n_used, n_tiles * tm


def _dispatch_kernel(slot_ref, x_hbm, xs_in_hbm, xs_hbm, sem, *, rows):
    del xs_in_hbm
    i = pl.program_id(0)
    base = i * rows

    def issue(r, c):
        for k in range(TOP_K):
            pltpu.make_async_copy(x_hbm.at[pl.ds(base + r, 1), :],
                                  xs_hbm.at[pl.ds(slot_ref[r * TOP_K + k], 1), :], sem).start()
        return c

    lax.fori_loop(0, rows, issue, 0)

    def drain_one_step():
        n = rows * TOP_K
        pltpu.make_async_copy(xs_hbm.at[pl.ds(0, n), :], xs_hbm.at[pl.ds(0, n), :], sem).wait()

    @pl.when(i > 0)
    def _():
        drain_one_step()

    @pl.when(i == pl.num_programs(0) - 1)
    def _():
        drain_one_step()


def _dispatch_rows(x1, slot, xs, rows):
    n = x1.shape[0]
    return pl.pallas_call(
        functools.partial(_dispatch_kernel, rows=rows),
        grid=(n // rows,),
        in_specs=[pl.BlockSpec((rows * TOP_K,), lambda i: (i,), memory_space=pltpu.SMEM),
                  pl.BlockSpec(memory_space=pl.ANY), pl.BlockSpec(memory_space=pl.ANY)],
        out_specs=pl.BlockSpec(memory_space=pl.ANY),
        out_shape=jax.ShapeDtypeStruct(xs.shape, xs.dtype),
        scratch_shapes=[pltpu.SemaphoreType.DMA(())],
        input_output_aliases={2: 0},
        compiler_params=_params(("arbitrary",), 16),
        name="moe_dispatch",
    )(slot, x1, xs)


def _ffn_kernel(te_ref, nu_ref, xs_ref, wg_ref, wu_ref, bg_ref, bu_ref, wd_ref, bd_ref, o_ref):
    i = pl.program_id(0)

    @pl.when(i < nu_ref[0])
    def _():
        x = xs_ref[...].astype(BF16)
        g = jnp.minimum(_dot(x, wg_ref[...]) + bg_ref[...], SWIGLU_LIMIT)
        u = jnp.clip(_dot(x, wu_ref[...]) + bu_ref[...], -SWIGLU_LIMIT, SWIGLU_LIMIT)
        act = (u + 1.0) * g * jax.nn.sigmoid(SWIGLU_ALPHA * g)
        o_ref[...] = _dot(act.astype(BF16), wd_ref[...]) + bd_ref[...]

    @pl.when(i >= nu_ref[0])
    def _():
        o_ref[...] = jnp.zeros(o_ref.shape, F32)


def _expert_ffn(xs, tile_e, n_used, wg, wu, bg, bu, wd, bd, tm):
    n_slots = xs.shape[0]
    row = lambda i, te, nu: (i, 0)
    wsel = lambda i, te, nu: (te[i], 0, 0)
    grid_spec = pltpu.PrefetchScalarGridSpec(
        num_scalar_prefetch=2,
        grid=(n_slots // tm,),
        in_specs=[pl.BlockSpec((tm, D_MODEL), row),
                  pl.BlockSpec((None, D_MODEL, D_FF), wsel), pl.BlockSpec((None, D_MODEL, D_FF), wsel),
                  pl.BlockSpec((None, 1, D_FF), wsel), pl.BlockSpec((None, 1, D_FF), wsel),
                  pl.BlockSpec((None, D_FF, D_MODEL), wsel), pl.BlockSpec((None, 1, D_MODEL), wsel)],
        out_specs=pl.BlockSpec((tm, D_MODEL), row),
    )
    return pl.pallas_call(
        _ffn_kernel,
        grid_spec=grid_spec,
        out_shape=jax.ShapeDtypeStruct((n_slots, D_MODEL), F32),
        compiler_params=_params(("arbitrary",), 48),
        name="moe_ffn",
    )(tile_e, n_used, xs, wg, wu, bg, bu, wd, bd)


def _combine_kernel(slot_ref, x1_ref, gate_ref, y_hbm, g_ref, b_ref, o_ref, buf, sem, *, rows, alpha):
    def issue(r, c):
        for k in range(TOP_K):
            pltpu.make_async_copy(y_hbm.at[pl.ds(slot_ref[r * TOP_K + k], 1), :],
                                  buf.at[k, pl.ds(r, 1), :], sem).start()
        return c

    lax.fori_loop(0, rows, issue, 0)
    for k in range(TOP_K):
        pltpu.make_async_copy(y_hbm.at[pl.ds(0, rows), :], buf.at[k], sem).wait()
    gate = gate_ref[...]
    f = ((gate[:, 0:1] * buf[0] + gate[:, 1:2] * buf[1]) + (gate[:, 2:3] * buf[2] + gate[:, 3:4] * buf[3]))
    z = alpha * x1_ref[...] + f
    mu = jnp.mean(z, axis=-1, keepdims=True)
    zc = z - mu
    var = jnp.mean(zc * zc, axis=-1, keepdims=True)
    o_ref[...] = zc * lax.rsqrt(var + LN_EPS) * g_ref[...] + b_ref[...]


def _combine(x1, gate, slot, yb, ln_g, ln_b, rows, alpha):
    n = x1.shape[0]
    row = lambda i: (i, 0)
    vec = pl.BlockSpec((1, D_MODEL), lambda i: (0, 0))
    return pl.pallas_call(
        functools.partial(_combine_kernel, rows=rows, alpha=alpha),
        grid=(n // rows,),
        in_specs=[pl.BlockSpec((rows * TOP_K,), lambda i: (i,), memory_space=pltpu.SMEM),
                  pl.BlockSpec((rows, D_MODEL), row), pl.BlockSpec((rows, LANES), row),
                  pl.BlockSpec(memory_space=pl.ANY), vec, vec],
        out_specs=pl.BlockSpec((rows, D_MODEL), row),
        out_shape=jax.ShapeDtypeStruct((n, D_MODEL), F32),
        scratch_shapes=[pltpu.VMEM((TOP_K, rows, D_MODEL), F32), pltpu.SemaphoreType.DMA(())],
        compiler_params=_params(("arbitrary",), 32),
        name="moe_combine",
    )(slot, x1, gate, yb, ln_g, ln_b)


def kernel(x_prompt, x_sample, cache_k, cache_v, state_conv, page_table, w_in, lambda_q1, lambda_k1,
           lambda_q2, lambda_k2, subln_g, w_attn_out, w_conv, b_conv, w_conv_out, w_o, ln1_g, ln1_b,
           w_router, b_router, w_gate_up, b_gate_up, w_down, b_down, ln2_g, ln2_b):
    bp, s, d = x_prompt.shape
    bd, t_new, _ = x_sample.shape
    depth = w_in.shape[0]
    n_phys = cache_k.shape[1]
    past = page_table.shape[1] * PAGE_SIZE
    alpha = (2.0 * depth) ** 0.25
    np_rows, ns_rows = bp * s, bd * t_new
    tm_p = 512
    tq = 512

    tab_p = _rope_tables(jnp.arange(s))
    tab_s = _rope_tables(jnp.tile(past + jnp.arange(t_new), bd))

    cache_kb = cache_k.reshape(depth * n_phys, PAGE_SIZE, QK_WIDTH).astype(BF16)
    cache_vb = cache_v.reshape(depth * n_phys, PAGE_SIZE, ATTN_WIDTH).astype(BF16)
    hp = x_prompt.reshape(np_rows, d)
    hs = x_sample.reshape(ns_rows, d)
    outs = {k: [] for k in ("kp", "vp", "cp", "ks", "vs", "cs")}
    for l in range(depth):
        lam_init = 0.8 - 0.6 * math.exp(-0.3 * l)
        lp = jnp.concatenate([lambda_q1[l:l + 1], lambda_k1[l:l + 1], lambda_q2[l:l + 1], lambda_k2[l:l + 1]], axis=0)
        w_l = w_in[l]
        w_qkv = w_l[:, :QKV_COLS].astype(BF16)
        w_rest = w_l[:, QKV_COLS:].astype(BF16)
        w_r = jnp.pad(w_router[l], ((0, 0), (0, LANES - N_EXPERTS))).astype(BF16)
        b_r = jnp.pad(b_router[l], (0, LANES - N_EXPERTS)).reshape(1, LANES)
        weights = (w_rest, w_attn_out[l].astype(BF16), w_conv_out[l].astype(BF16), w_o[l].astype(BF16), w_r)
        vecs = (subln_g[l].reshape(1, V_DIM), w_conv[l],
                b_conv[l].reshape(1, CONV_CH), ln1_g[l].reshape(1, d), ln1_b[l].reshape(1, d), b_r)

        q1, q2, kf, kb, vf, vt = _qkv_proj(hp, w_qkv, tab_p, tm_p)
        shp = (bp, s, QK_WIDTH)
        o_p = _flash_attention(lp, q1.reshape(shp), q2.reshape(shp), kb.reshape(shp), vt, lam_init, tq)
        x1_p, ti_p, tg_p, nc_p = _post_block(
            hp, o_p.reshape(np_rows, ATTN_WIDTH), weights, vecs, tm=tm_p // 2, tiles_per_seq=s // (tm_p // 2),
            conv_state=jnp.zeros((bp, CONV_K - 1, CONV_CH), F32), alpha=alpha, lam_init=lam_init)
        outs["kp"].append(kf.reshape(bp, s, N_HEADS, 2, HEAD_DIM))
        outs["vp"].append(vf.reshape(bp, s, N_HEADS, V_DIM))
        outs["cp"].append(nc_p)

        q1s, q2s, kfs, _, vfs, _ = _qkv_proj(hs, w_qkv, tab_s, ns_rows)
        seq3 = lambda a: a.reshape(bd, t_new, QK_WIDTH)
        pad_page = lambda a: jnp.pad(seq3(a), ((0, 0), (0, PAGE_SIZE - t_new), (0, 0)))
        o_s = _paged_attention(
            lp, page_table + l * n_phys, seq3(q1s).astype(F32), seq3(q2s).astype(F32), pad_page(kfs), pad_page(vfs),
            cache_kb, cache_vb, lam_init)
        st = state_conv[l]
        zero = jnp.zeros((bd, 1, CONV_CH), F32)
        hist1 = jnp.concatenate([st[:, 1:2]] + [zero] * (t_new - 1), axis=1).reshape(ns_rows, CONV_CH)
        hist2 = jnp.concatenate([st[:, 0:1], st[:, 1:2]] + [zero] * (t_new - 2), axis=1).reshape(ns_rows, CONV_CH)
        x1_s, ti_s, tg_s, u_s = _post_block(
            hs, o_s.reshape(ns_rows, ATTN_WIDTH), weights, vecs, tm=ns_rows, hist=(hist1, hist2), t_short=t_new,
            alpha=alpha, lam_init=lam_init)
        outs["ks"].append(kfs.reshape(bd, t_new, N_HEADS, 2, HEAD_DIM))
        outs["vs"].append(vfs.reshape(bd, t_new, N_HEADS, V_DIM))
        outs["cs"].append(u_s.reshape(bd, t_new, CONV_CH)[:, t_new - (CONV_K - 1):])

        top_i = jnp.concatenate([ti_p[:, :TOP_K], ti_s[:, :TOP_K]], axis=0)
        slot, tile_e, n_used, n_slots = _route(top_i, FFN_TILE)
        slot_p, slot_s = slot[:np_rows * TOP_K], slot[np_rows * TOP_K:]
        xs = _dispatch_rows(x1_p, slot_p, jnp.zeros((n_slots, d), F32), 512)
        xs = _dispatch_rows(x1_s, slot_s, xs, ns_rows)
        wgu = w_gate_up[l]
        bgu = b_gate_up[l]
        yb = _expert_ffn(
            xs, tile_e, n_used, wgu[:, :, 0::2].astype(BF16), wgu[:, :, 1::2].astype(BF16),
            bgu[:, None, 0::2], bgu[:, None, 1::2], w_down[l].astype(BF16), b_down[l][:, None, :], FFN_TILE)
        g2, b2 = ln2_g[l].reshape(1, d), ln2_b[l].reshape(1, d)
        hp = _combine(x1_p, tg_p, slot_p, yb, g2, b2, 256, alpha)
        hs = _combine(x1_s, tg_s, slot_s, yb, g2, b2, ns_rows, alpha)

    return (hp.reshape(bp, s, d), hs.reshape(bd, t_new, d),
            jnp.stack(outs["kp"]), jnp.stack(outs["vp"]), jnp.stack(outs["cp"]),
            jnp.stack(outs["ks"]), jnp.stack(outs["vs"]), jnp.stack(outs["cs"]))
```

```python
import functools
import math

import jax
import jax.numpy as jnp
from jax import lax
from jax.experimental import pallas as pl
from jax.experimental.pallas import tpu as pltpu

F32 = jnp.float32
BF16 = jnp.bfloat16

D_MODEL = 1024
N_HEADS = 8
HEAD_DIM = 64
V_DIM = 2 * HEAD_DIM
QK_WIDTH = N_HEADS * 2 * HEAD_DIM
ATTN_WIDTH = N_HEADS * V_DIM
ROT_DIM = HEAD_DIM // 4
ROPE_THETA = 500000.0
CONV_CH = D_MODEL // 2
CONV_K = 3
N_EXPERTS = 32
TOP_K = 4
D_FF = D_MODEL
SWIGLU_LIMIT = 7.0
SWIGLU_ALPHA = 1.702
LN_EPS = 1e-5
PAGE_SIZE = 128
LANES = 128
MIB = 1024 * 1024

QKV_COLS = 2 * QK_WIDTH + ATTN_WIDTH
REST_COLS = 3 * CONV_CH + 2 * D_MODEL

PAGES_PER_STEP = 8
FFN_TILE = 512


def _params(semantics, vmem_mib):
    return pltpu.CompilerParams(dimension_semantics=semantics, vmem_limit_bytes=vmem_mib * MIB)


def _dot(a, b):
    return jnp.dot(a, b, preferred_element_type=F32)


def _dot_nt(a, b):
    return lax.dot_general(a, b, (((1,), (1,)), ((), ())), preferred_element_type=F32)


def _rope_tables(pos):
    inv = ROPE_THETA ** (-jnp.arange(0, ROT_DIM, 2, dtype=F32) / ROT_DIM)
    ang = pos.astype(F32)[:, None] * inv[None, :]
    cos, sin = jnp.cos(ang), jnp.sin(ang)
    n = pos.shape[0]
    half = ROT_DIM // 2
    rest = HEAD_DIM - ROT_DIM
    c = jnp.concatenate([cos, cos, jnp.ones((n, rest), F32)], axis=1)
    s_up = jnp.concatenate([-sin, jnp.zeros((n, half + rest), F32)], axis=1)
    s_dn = jnp.concatenate([jnp.zeros((n, half), F32), sin, jnp.zeros((n, rest), F32)], axis=1)
    rep = LANES // HEAD_DIM
    return tuple(jnp.tile(t, (1, rep)) for t in (c, s_up, s_dn))


def _qkv_kernel(x_ref, w_ref, c_ref, su_ref, sd_ref, q1_ref, q2_ref, kf_ref, kb_ref, vf_ref, vt_ref):
    x = x_ref[...].astype(BF16)
    c, s_up, s_dn = c_ref[...], su_ref[...], sd_ref[...]
    half = ROT_DIM // 2
    first_map = lax.broadcasted_iota(jnp.int32, c.shape, 1) < HEAD_DIM
    scale = HEAD_DIM ** -0.5

    def rope(blk):
        return blk * c + pltpu.roll(blk, LANES - half, 1) * s_up + pltpu.roll(blk, half, 1) * s_dn

    hq = _dot(x, w_ref[:, 0:QK_WIDTH])
    for j in range(QK_WIDTH // LANES):
        sl = slice(LANES * j, LANES * (j + 1))
        r = rope(hq[:, sl]) * scale
        q1_ref[:, sl] = jnp.where(first_map, r, 0.0).astype(BF16)
        q2_ref[:, sl] = jnp.where(first_map, 0.0, r).astype(BF16)
    hk = _dot(x, w_ref[:, QK_WIDTH:2 * QK_WIDTH])
    for j in range(QK_WIDTH // LANES):
        sl = slice(LANES * j, LANES * (j + 1))
        r = rope(hk[:, sl])
        kf_ref[:, sl] = r
        kb_ref[:, sl] = r.astype(BF16)
    hv = _dot(x, w_ref[:, 2 * QK_WIDTH:QKV_COLS])
    vf_ref[...] = hv
    vt_ref[...] = hv.T.astype(BF16)


def _qkv_proj(x2d, w_qkv, tables, tm):
    n = x2d.shape[0]
    seq = tables[0].shape[0]
    n_pos_blocks = seq // tm
    row = lambda i: (i, 0)
    tab = pl.BlockSpec((tm, LANES), lambda i: (i % n_pos_blocks, 0))
    wide = lambda: pl.BlockSpec((tm, QK_WIDTH), row)
    shp = lambda dt: jax.ShapeDtypeStruct((n, QK_WIDTH), dt)
    vt_spec = pl.BlockSpec((None, ATTN_WIDTH, tm), lambda i: (i // n_pos_blocks, 0, i % n_pos_blocks))
    return pl.pallas_call(
        _qkv_kernel,
        grid=(n // tm,),
        in_specs=[pl.BlockSpec((tm, D_MODEL), row),
                  pl.BlockSpec((D_MODEL, QKV_COLS), lambda i: (0, 0)),
                  tab, tab, tab],
        out_specs=[wide() for _ in range(5)] + [vt_spec],
        out_shape=[shp(BF16), shp(BF16), shp(F32), shp(BF16), shp(F32),
                   jax.ShapeDtypeStruct((n // seq, ATTN_WIDTH, seq), BF16)],
        compiler_params=_params(("parallel",), 48),
        name="qkv_proj",
    )(x2d, w_qkv, *tables)


def _lambda(lp, lam_init):
    a = jnp.sum(lp[0:1] * lp[1:2], axis=-1, keepdims=True)
    b = jnp.sum(lp[2:3] * lp[3:4], axis=-1, keepdims=True)
    return jnp.exp(a) - jnp.exp(b) + lam_init


FLASH_GROUP = 256


def _flash_kernel(lp_ref, q1_ref, q2_ref, k_ref, vt_ref, o_ref, m_sc, l_sc, acc_sc, *, tq, lam_init):
    qi = pl.program_id(2)
    m_sc[...] = jnp.full(m_sc.shape, -jnp.inf, F32)
    l_sc[...] = jnp.zeros(l_sc.shape, F32)
    acc_sc[...] = jnp.zeros(acc_sc.shape, F32)
    gw = FLASH_GROUP
    groups = [(m, c0) for m in range(2) for c0 in range(0, tq, gw)]

    def block(ki, masked):
        start = pl.multiple_of(ki * tq, tq)
        k = k_ref[pl.ds(start, tq), :]
        vt = vt_ref[:, pl.ds(start, tq)]
        scores = [_dot_nt(k, q_ref[...]) for q_ref in (q1_ref, q2_ref)]
        probs, alphas = [], []
        for m, c0 in groups:
            cols = slice(m * tq + c0, m * tq + c0 + gw)
            s = scores[m][:, c0:c0 + gw]
            if masked:
                key = lax.broadcasted_iota(jnp.int32, s.shape, 0)
                qry = lax.broadcasted_iota(jnp.int32, s.shape, 1) + c0
                s = jnp.where(key <= qry, s, -jnp.inf)
            m_prev = m_sc[:, cols]
            m_new = jnp.maximum(m_prev, jnp.max(s, axis=0, keepdims=True))
            alpha = jnp.exp(m_prev - m_new)
            p = jnp.exp(s - m_new)
            l_sc[:, cols] = alpha * l_sc[:, cols] + jnp.sum(p, axis=0, keepdims=True)
            m_sc[:, cols] = m_new
            probs.append(p.astype(BF16))
            alphas.append(alpha)
        for m in range(2):
            cols = slice(m * tq, (m + 1) * tq)
            per_map = tq // gw
            p = jnp.concatenate(probs[m * per_map:(m + 1) * per_map], axis=1)
            alpha = jnp.concatenate(alphas[m * per_map:(m + 1) * per_map], axis=1)
            acc_sc[:, cols] = alpha * acc_sc[:, cols] + _dot(vt, p)

    def body(ki, carry):
        block(ki, False)
        return carry

    lax.fori_loop(0, qi, body, 0)
    block(qi, True)
    o = acc_sc[...] / l_sc[...]
    lam = _lambda(lp_ref[...], lam_init)
    o_ref[...] = (o[:, :tq] - lam * o[:, tq:]).T


def _flash_attention(lp, q1, q2, kb, vt, lam_init, tq):
    b, s, _ = q1.shape
    qspec = pl.BlockSpec((None, tq, LANES), lambda bi, h, qi: (bi, qi, h))
    kspec = pl.BlockSpec((None, s, LANES), lambda bi, h, qi: (bi, 0, h))
    vspec = pl.BlockSpec((None, V_DIM, s), lambda bi, h, qi: (bi, h, 0))
    return pl.pallas_call(
        functools.partial(_flash_kernel, tq=tq, lam_init=lam_init),
        grid=(b, N_HEADS, s // tq),
        in_specs=[pl.BlockSpec((4, HEAD_DIM), lambda bi, h, qi: (0, 0)), qspec, qspec, kspec, vspec],
        out_specs=qspec,
        out_shape=jax.ShapeDtypeStruct((b, s, ATTN_WIDTH), F32),
        scratch_shapes=[pltpu.VMEM((1, 2 * tq), F32), pltpu.VMEM((1, 2 * tq), F32),
                        pltpu.VMEM((V_DIM, 2 * tq), F32)],
        compiler_params=_params(("parallel", "parallel", "arbitrary"), 48),
        name="flash_diff_attn",
    )(lp, q1, q2, kb, vt)


def _paged_kernel(pt_ref, lp_ref, q1_ref, q2_ref, kn_ref, vn_ref, *rest, t_new, lam_init):
    np_ = PAGES_PER_STEP
    k_pages, v_pages = rest[:np_], rest[np_:2 * np_]
    o_ref, m_sc, l_sc, acc_sc, kc_sc, vc_sc = rest[2 * np_:]
    j = pl.program_id(1)
    rows = 2 * t_new * N_HEADS

    @pl.when(j == 0)
    def _():
        m_sc[...] = jnp.full(m_sc.shape, -jnp.inf, F32)
        l_sc[...] = jnp.zeros(l_sc.shape, F32)
        acc_sc[...] = jnp.zeros(acc_sc.shape, F32)

    head_of_lane = lax.broadcasted_iota(jnp.int32, (N_HEADS, QK_WIDTH), 1) // LANES
    own_head = head_of_lane == lax.broadcasted_iota(jnp.int32, (N_HEADS, QK_WIDTH), 0)
    pieces = []
    for q_ref in (q1_ref, q2_ref):
        for t in range(t_new):
            qt = jnp.broadcast_to(q_ref[t:t + 1, :], (N_HEADS, QK_WIDTH))
            pieces.append(jnp.where(own_head, qt, 0.0))
    qbd = jnp.concatenate(pieces, axis=0).astype(BF16)

    def update(s, v):
        m_prev = m_sc[...]
        m_new = jnp.maximum(m_prev, jnp.max(s, axis=-1, keepdims=True))
        alpha = jnp.exp(m_prev - m_new)
        p = jnp.exp(s - m_new)
        l_sc[...] = alpha * l_sc[...] + jnp.sum(p, axis=-1, keepdims=True)
        acc_sc[...] = alpha * acc_sc[...] + _dot(p.astype(BF16), v)
        m_sc[...] = m_new

    for i in range(np_):
        kc_sc[PAGE_SIZE * i:PAGE_SIZE * (i + 1), :] = k_pages[i][...]
        for h in range(N_HEADS):
            vh = v_pages[i][pl.ds(h, PAGE_SIZE, stride=N_HEADS), :]
            vc_sc[PAGE_SIZE * i:PAGE_SIZE * (i + 1), V_DIM * h:V_DIM * (h + 1)] = vh.astype(BF16)
    update(_dot_nt(qbd, kc_sc[...]), vc_sc[...])

    @pl.when(j == pl.num_programs(1) - 1)
    def _():
        s = _dot_nt(qbd, kn_ref[...].astype(BF16))
        r = lax.broadcasted_iota(jnp.int32, s.shape, 0)
        tok = (r // N_HEADS) % t_new
        col = lax.broadcasted_iota(jnp.int32, s.shape, 1)
        s = jnp.where(col <= tok, s, -jnp.inf)
        update(s, vn_ref[...].astype(BF16))
        o = acc_sc[...] / l_sc[...]
        lam = _lambda(lp_ref[...], lam_init)
        half = rows // 2
        d = o[:half] - lam * o[half:]
        for t in range(t_new):
            blk = jnp.where(own_head, d[N_HEADS * t:N_HEADS * (t + 1)], 0.0)
            o_ref[t:t + 1, :] = jnp.sum(blk, axis=0, keepdims=True)


def _paged_attention(lp, page_table, q1, q2, k_new, v_new, cache_k2, cache_v2, lam_init):
    bd, t_new, _ = q1.shape
    n_pages = page_table.shape[1]
    np_ = PAGES_PER_STEP
    rows = 2 * t_new * N_HEADS
    chunk = np_ * PAGE_SIZE
    seq = lambda b, j, pt: (b, 0, 0)
    qspec = pl.BlockSpec((None, t_new, QK_WIDTH), seq)
    nspec = pl.BlockSpec((None, PAGE_SIZE, QK_WIDTH), seq)

    def k_page_spec(i):
        return pl.BlockSpec((None, PAGE_SIZE, QK_WIDTH), lambda b, j, pt: (pt[b, j * np_ + i], 0, 0))

    def v_page_spec(i):
        return pl.BlockSpec((PAGE_SIZE * N_HEADS, V_DIM), lambda b, j, pt: (pt[b, j * np_ + i], 0))

    grid_spec = pltpu.PrefetchScalarGridSpec(
        num_scalar_prefetch=1,
        grid=(bd, n_pages // np_),
        in_specs=([pl.BlockSpec((4, HEAD_DIM), lambda b, j, pt: (0, 0)), qspec, qspec, nspec, nspec]
                  + [k_page_spec(i) for i in range(np_)] + [v_page_spec(i) for i in range(np_)]),
        out_specs=qspec,
        scratch_shapes=[pltpu.VMEM((rows, 1), F32), pltpu.VMEM((rows, 1), F32),
                        pltpu.VMEM((rows, ATTN_WIDTH), F32),
                        pltpu.VMEM((chunk, QK_WIDTH), BF16), pltpu.VMEM((chunk, ATTN_WIDTH), BF16)],
    )
    return pl.pallas_call(
        functools.partial(_paged_kernel, t_new=t_new, lam_init=lam_init),
        grid_spec=grid_spec,
        out_shape=jax.ShapeDtypeStruct((bd, t_new, ATTN_WIDTH), F32),
        compiler_params=_params(("parallel", "arbitrary"), 48),
        name="paged_diff_attn",
    )(page_table, lp, q1, q2, k_new, v_new, *([cache_k2] * np_), *([cache_v2] * np_))


def _post_kernel(*refs, tm, tiles_per_seq, t_short, alpha, lam_init):
    long_seq = tiles_per_seq > 0
    (x_ref, o_ref, w2_ref, wa_ref, wc_ref, wo_ref, wr_ref, g_ref, cw_ref, cb_ref,
     l1g_ref, l1b_ref, br_ref) = refs[:13]
    if long_seq:
        st_ref, x1_ref, ti_ref, tg_ref, nc_ref, carry = refs[13:]
    else:
        h1_ref, h2_ref, x1_ref, ti_ref, tg_ref, u_ref = refs[13:]

    x = x_ref[...]
    h = _dot(x.astype(BF16), w2_ref[...])
    c0 = CONV_CH
    bg, cg, xc = h[:, 0:c0], h[:, c0:2 * c0], h[:, 2 * c0:3 * c0]
    ga = h[:, 3 * c0:3 * c0 + D_MODEL]
    gc = h[:, 3 * c0 + D_MODEL:]

    u = cg * xc
    row = lax.broadcasted_iota(jnp.int32, u.shape, 0)
    if long_seq:
        first = pl.program_id(0) % tiles_per_seq == 0
        st = st_ref[...]
        prev2 = jnp.where(first, st[0:1], carry[0:1])
        prev1 = jnp.where(first, st[1:2], carry[1:2])
        u1 = jnp.where(row == 0, prev1, pltpu.roll(u, 1, 0))
        u2 = jnp.where(row == 0, prev2, jnp.where(row == 1, prev1, pltpu.roll(u, 2, 0)))
        carry[0:2, :] = u[tm - 2:tm]
        nc_ref[...] = u[tm - 2:tm]
    else:
        tpos = row % t_short
        u1 = jnp.where(tpos >= 1, pltpu.roll(u, 1, 0), h1_ref[...])
        u2 = jnp.where(tpos >= 2, pltpu.roll(u, 2, 0), h2_ref[...])
        u_ref[...] = u
    cw = cw_ref[...]
    y_c = bg * (cb_ref[...] + u2 * cw[0:1] + u1 * cw[1:2] + u * cw[2:3])
    y_c = _dot(y_c.astype(BF16), wc_ref[...])

    o = o_ref[...]
    gain = g_ref[...]
    heads = []
    for hd in range(N_HEADS):
        oh = o[:, V_DIM * hd:V_DIM * (hd + 1)]
        ms = jnp.mean(oh * oh, axis=-1, keepdims=True)
        heads.append(oh * lax.rsqrt(ms + LN_EPS) * gain * (1.0 - lam_init))
    y_a = _dot(jnp.concatenate(heads, axis=1).astype(BF16), wa_ref[...])

    mix = jax.nn.sigmoid(ga) * y_a + jax.nn.sigmoid(gc) * y_c
    z = alpha * x + _dot(mix.astype(BF16), wo_ref[...])
    mu = jnp.mean(z, axis=-1, keepdims=True)
    zc = z - mu
    var = jnp.mean(zc * zc, axis=-1, keepdims=True)
    x1 = zc * lax.rsqrt(var + LN_EPS) * l1g_ref[...] + l1b_ref[...]
    x1_ref[...] = x1

    logits = _dot(x1.astype(BF16), wr_ref[...]) + br_ref[...]
    lane = lax.broadcasted_iota(jnp.int32, logits.shape, 1)
    logits = jnp.where(lane < N_EXPERTS, logits, -jnp.inf)
    idx_out = jnp.zeros(logits.shape, jnp.int32)
    val_out = jnp.zeros(logits.shape, F32)
    top0 = None
    denom = jnp.zeros((tm, 1), F32)
    for k in range(TOP_K):
        mx = jnp.max(logits, axis=-1, keepdims=True)
        idx = jnp.min(jnp.where(logits == mx, lane, LANES), axis=-1, keepdims=True)
        if k == 0:
            top0 = mx
        e = jnp.exp(mx - top0)
        denom = denom + e
        idx_out = jnp.where(lane == k, idx, idx_out)
        val_out = jnp.where(lane == k, e, val_out)
        logits = jnp.where(lane == idx, -jnp.inf, logits)
    ti_ref[...] = idx_out
    tg_ref[...] = val_out / denom


def _post_block(x2d, o2d, weights, vecs, *, tm, tiles_per_seq=0, conv_state=None, hist=None, t_short=0,
                alpha, lam_init):
    n = x2d.shape[0]
    row = lambda i: (i, 0)
    full = lambda a: pl.BlockSpec(a.shape, lambda i: (0,) * a.ndim)
    in_specs = [pl.BlockSpec((tm, D_MODEL), row), pl.BlockSpec((tm, ATTN_WIDTH), row)]
    in_specs += [full(w) for w in weights] + [full(v) for v in vecs]
    out_specs = [pl.BlockSpec((tm, D_MODEL), row), pl.BlockSpec((tm, LANES), row), pl.BlockSpec((tm, LANES), row)]
    out_shape = [jax.ShapeDtypeStruct((n, D_MODEL), F32), jax.ShapeDtypeStruct((n, LANES), jnp.int32),
                 jax.ShapeDtypeStruct((n, LANES), F32)]
    scratch = []
    if tiles_per_seq > 0:
        extra = [conv_state]
        in_specs.append(pl.BlockSpec((None, CONV_K - 1, CONV_CH), lambda i: (i // tiles_per_seq, 0, 0)))
        out_specs.append(pl.BlockSpec((None, CONV_K - 1, CONV_CH), lambda i: (i // tiles_per_seq, 0, 0)))
        out_shape.append(jax.ShapeDtypeStruct(conv_state.shape, F32))
        scratch.append(pltpu.VMEM((8, CONV_CH), F32))
    else:
        extra = list(hist)
        in_specs += [pl.BlockSpec((tm, CONV_CH), row)] * 2
        out_specs.append(pl.BlockSpec((tm, CONV_CH), row))
        out_shape.append(jax.ShapeDtypeStruct((n, CONV_CH), F32))
    return pl.pallas_call(
        functools.partial(_post_kernel, tm=tm, tiles_per_seq=tiles_per_seq, t_short=t_short, alpha=alpha,
                          lam_init=lam_init),
        grid=(n // tm,),
        in_specs=in_specs,
        out_specs=out_specs,
        out_shape=out_shape,
        scratch_shapes=scratch,
        compiler_params=_params(("arbitrary",), 56),
        name="post_block",
    )(x2d, o2d, *weights, *vecs, *extra)


def _route(top_i, tm):
    n = top_i.shape[0]
    a = n * TOP_K
    flat_e = top_i.reshape(a)
    onehot = (flat_e[:, None] == jnp.arange(N_EXPERTS, dtype=jnp.int32)[None, :]).astype(jnp.int32)
    csum = jnp.cumsum(onehot, axis=0)
    rank = jnp.take_along_axis(csum, flat_e[:, None], axis=1)[:, 0] - 1
    counts = csum[-1]
    padded = (counts + tm - 1) // tm * tm
    pad_end = jnp.cumsum(padded)
    pad_start = pad_end - padded
    slot = (pad_start[flat_e] + rank).astype(jnp.int32)
    n_tiles = (a + N_EXPERTS * (tm - 1) + tm - 1) // tm
    tile_start = jnp.arange(n_tiles, dtype=jnp.int32) * tm
    tile_e = jnp.minimum(jnp.sum((pad_end[None, :] <= tile_start[:, None]).astype(jnp.int32), axis=1),
                         N_EXPERTS - 1).astype(jnp.int32)
    n_used = (pad_end[-1] // tm).astype(jnp.int32).reshape(1)
    return slot, tile_e, n_used, n_tiles * tm


def _dispatch_kernel(slot_ref, x_ref, xs_in_hbm, xs_hbm, sem, *, rows):
    del xs_in_hbm

    def issue(r, c):
        for k in range(TOP_K):
            pltpu.make_async_copy(x_ref.at[pl.ds(r, 1), :],
                                  xs_hbm.at[pl.ds(slot_ref[r * TOP_K + k], 1), :], sem).start()
        return c

    lax.fori_loop(0, rows, issue, 0)
    for _ in range(TOP_K):
        pltpu.make_async_copy(x_ref, xs_hbm.at[pl.ds(0, rows), :], sem).wait()


def _dispatch_rows(x1, slot, xs, rows):
    n = x1.shape[0]
    return pl.pallas_call(
        functools.partial(_dispatch_kernel, rows=rows),
        grid=(n // rows,),
        in_specs=[pl.BlockSpec((rows * TOP_K,), lambda i: (i,), memory_space=pltpu.SMEM),
                  pl.BlockSpec((rows, D_MODEL), lambda i: (i, 0)), pl.BlockSpec(memory_space=pl.ANY)],
        out_specs=pl.BlockSpec(memory_space=pl.ANY),
        out_shape=jax.ShapeDtypeStruct(xs.shape, xs.dtype),
        scratch_shapes=[pltpu.SemaphoreType.DMA(())],
        input_output_aliases={2: 0},
        compiler_params=_params(("arbitrary",), 16),
        name="moe_dispatch",
    )(slot, x1, xs)


def _split_kernel(w_ref, p_ref, g_ref, u_ref):
    w = w_ref[...].astype(BF16)
    half = 2 * LANES
    for t in range(D_FF // half):
        y = _dot(w[:, 2 * half * t:2 * half * (t + 1)], p_ref[...])
        g_ref[:, half * t:half * (t + 1)] = y[:, :half].astype(BF16)
        u_ref[:, half * t:half * (t + 1)] = y[:, half:].astype(BF16)


def _split_gate_up(w_gate_up, rows):
    e, dm, f2 = w_gate_up.shape
    half = 2 * LANES
    r = lax.broadcasted_iota(jnp.int32, (2 * half, 2 * half), 0)
    c = lax.broadcasted_iota(jnp.int32, (2 * half, 2 * half), 1)
    perm = jnp.where(c < half, r == 2 * c, r == 2 * (c - half) + 1).astype(BF16)
    out = pl.BlockSpec((None, rows, f2 // 2), lambda ei, i: (ei, i, 0))
    return pl.pallas_call(
        _split_kernel,
        grid=(e, dm // rows),
        in_specs=[pl.BlockSpec((None, rows, f2), lambda ei, i: (ei, i, 0)),
                  pl.BlockSpec((2 * half, 2 * half), lambda ei, i: (0, 0))],
        out_specs=[out, out],
        out_shape=[jax.ShapeDtypeStruct((e, dm, f2 // 2), BF16)] * 2,
        compiler_params=_params(("parallel", "parallel"), 32),
        name="split_gate_up",
    )(w_gate_up, perm)


def _ffn_kernel(te_ref, nu_ref, xs_ref, wg_ref, wu_ref, bg_ref, bu_ref, wd_ref, bd_ref, o_ref):
    i = pl.program_id(0)

    @pl.when(i < nu_ref[0])
    def _():
        x = xs_ref[...].astype(BF16)
        g = jnp.minimum(_dot(x, wg_ref[...]) + bg_ref[...], SWIGLU_LIMIT)
        u = jnp.clip(_dot(x, wu_ref[...]) + bu_ref[...], -SWIGLU_LIMIT, SWIGLU_LIMIT)
        act = (u + 1.0) * g * jax.nn.sigmoid(SWIGLU_ALPHA * g)
        o_ref[...] = _dot(act.astype(BF16), wd_ref[...]) + bd_ref[...]

    @pl.when(i >= nu_ref[0])
    def _():
        o_ref[...] = jnp.zeros(o_ref.shape, F32)


def _expert_ffn(xs, tile_e, n_used, wg, wu, bg, bu, wd, bd, tm):
    n_slots = xs.shape[0]
    row = lambda i, te, nu: (i, 0)
    wsel = lambda i, te, nu: (te[i], 0, 0)
    grid_spec = pltpu.PrefetchScalarGridSpec(
        num_scalar_prefetch=2,
        grid=(n_slots // tm,),
        in_specs=[pl.BlockSpec((tm, D_MODEL), row),
                  pl.BlockSpec((None, D_MODEL, D_FF), wsel), pl.BlockSpec((None, D_MODEL, D_FF), wsel),
                  pl.BlockSpec((None, 1, D_FF), wsel), pl.BlockSpec((None, 1, D_FF), wsel),
                  pl.BlockSpec((None, D_FF, D_MODEL), wsel), pl.BlockSpec((None, 1, D_MODEL), wsel)],
        out_specs=pl.BlockSpec((tm, D_MODEL), row),
    )
    return pl.pallas_call(
        _ffn_kernel,
        grid_spec=grid_spec,
        out_shape=jax.ShapeDtypeStruct((n_slots, D_MODEL), F32),
        compiler_params=_params(("arbitrary",), 48),
        name="moe_ffn",
    )(tile_e, n_used, xs, wg, wu, bg, bu, wd, bd)


def _combine_kernel(slot_ref, x1_ref, gate_ref, y_hbm, g_ref, b_ref, o_ref, buf, sem, *, rows, alpha):
    def issue(r, c):
        for k in range(TOP_K):
            pltpu.make_async_copy(y_hbm.at[pl.ds(slot_ref[r * TOP_K + k], 1), :],
                                  buf.at[k, pl.ds(r, 1), :], sem).start()
        return c

    lax.fori_loop(0, rows, issue, 0)
    for k in range(TOP_K):
        pltpu.make_async_copy(y_hbm.at[pl.ds(0, rows), :], buf.at[k], sem).wait()
    gate = gate_ref[...]
    f = ((gate[:, 0:1] * buf[0] + gate[:, 1:2] * buf[1]) + (gate[:, 2:3] * buf[2] + gate[:, 3:4] * buf[3]))
    z = alpha * x1_ref[...] + f
    mu = jnp.mean(z, axis=-1, keepdims=True)
    zc = z - mu
    var = jnp.mean(zc * zc, axis=-1, keepdims=True)
    o_ref[...] = zc * lax.rsqrt(var + LN_EPS) * g_ref[...] + b_ref[...]


def _combine(x1, gate, slot, yb, ln_g, ln_b, rows, alpha):
    n = x1.shape[0]
    row = lambda i: (i, 0)
    vec = pl.BlockSpec((1, D_MODEL), lambda i: (0, 0))
    return pl.pallas_call(
        functools.partial(_combine_kernel, rows=rows, alpha=alpha),
        grid=(n // rows,),
        in_specs=[pl.BlockSpec((rows * TOP_K,), lambda i: (i,), memory_space=pltpu.SMEM),
                  pl.BlockSpec((rows, D_MODEL), row), pl.BlockSpec((rows, LANES), row),
                  pl.BlockSpec(memory_space=pl.ANY), vec, vec],
        out_specs=pl.BlockSpec((rows, D_MODEL), row),
        out_shape=jax.ShapeDtypeStruct((n, D_MODEL), F32),
        scratch_shapes=[pltpu.VMEM((TOP_K, rows, D_MODEL), F32), pltpu.SemaphoreType.DMA(())],
        compiler_params=_params(("arbitrary",), 32),
        name="moe_combine",
    )(slot, x1, gate, yb, ln_g, ln_b)


def kernel(x_prompt, x_sample, cache_k, cache_v, state_conv, page_table, w_in, lambda_q1, lambda_k1,
           lambda_q2, lambda_k2, subln_g, w_attn_out, w_conv, b_conv, w_conv_out, w_o, ln1_g, ln1_b,
           w_router, b_router, w_gate_up, b_gate_up, w_down, b_down, ln2_g, ln2_b):
    bp, s, d = x_prompt.shape
    bd, t_new, _ = x_sample.shape
    depth = w_in.shape[0]
    n_phys = cache_k.shape[1]
    past = page_table.shape[1] * PAGE_SIZE
    alpha = (2.0 * depth) ** 0.25
    np_rows, ns_rows = bp * s, bd * t_new
    tm_p = 512
    tq = 512

    tab_p = _rope_tables(jnp.arange(s))
    tab_s = _rope_tables(jnp.tile(past + jnp.arange(t_new), bd))

    cache_kb = cache_k.reshape(depth * n_phys, PAGE_SIZE, QK_WIDTH).astype(BF16)
    cache_v2 = cache_v.reshape(depth * n_phys * PAGE_SIZE * N_HEADS, V_DIM)
    hp = x_prompt.reshape(np_rows, d)
    hs = x_sample.reshape(ns_rows, d)
    outs = {k: [] for k in ("kp", "vp", "cp", "ks", "vs", "cs")}
    for l in range(depth):
        lam_init = 0.8 - 0.6 * math.exp(-0.3 * l)
        lp = jnp.concatenate([lambda_q1[l:l + 1], lambda_k1[l:l + 1], lambda_q2[l:l + 1], lambda_k2[l:l + 1]], axis=0)
        w_l = w_in[l]
        w_qkv = w_l[:, :QKV_COLS].astype(BF16)
        w_rest = w_l[:, QKV_COLS:].astype(BF16)
        w_r = jnp.pad(w_router[l], ((0, 0), (0, LANES - N_EXPERTS))).astype(BF16)
        b_r = jnp.pad(b_router[l], (0, LANES - N_EXPERTS)).reshape(1, LANES)
        weights = (w_rest, w_attn_out[l].astype(BF16), w_conv_out[l].astype(BF16), w_o[l].astype(BF16), w_r)
        vecs = (subln_g[l].reshape(1, V_DIM), w_conv[l],
                b_conv[l].reshape(1, CONV_CH), ln1_g[l].reshape(1, d), ln1_b[l].reshape(1, d), b_r)

        q1, q2, kf, kb, vf, vt = _qkv_proj(hp, w_qkv, tab_p, tm_p)
        shp = (bp, s, QK_WIDTH)
        o_p = _flash_attention(lp, q1.reshape(shp), q2.reshape(shp), kb.reshape(shp), vt, lam_init, tq)
        x1_p, ti_p, tg_p, nc_p = _post_block(
            hp, o_p.reshape(np_rows, ATTN_WIDTH), weights, vecs, tm=tm_p // 2, tiles_per_seq=s // (tm_p // 2),
            conv_state=jnp.zeros((bp, CONV_K - 1, CONV_CH), F32), alpha=alpha, lam_init=lam_init)
        outs["kp"].append(kf.reshape(bp, s, N_HEADS, 2, HEAD_DIM))
        outs["vp"].append(vf.reshape(bp, s, N_HEADS, V_DIM))
        outs["cp"].append(nc_p)

        q1s, q2s, kfs, _, vfs, _ = _qkv_proj(hs, w_qkv, tab_s, ns_rows)
        seq3 = lambda a: a.reshape(bd, t_new, QK_WIDTH)
        pad_page = lambda a: jnp.pad(seq3(a), ((0, 0), (0, PAGE_SIZE - t_new), (0, 0)))
        o_s = _paged_attention(
            lp, page_table + l * n_phys, seq3(q1s).astype(F32), seq3(q2s).astype(F32), pad_page(kfs), pad_page(vfs),
            cache_kb, cache_v2, lam_init)
        st = state_conv[l]
        zero = jnp.zeros((bd, 1, CONV_CH), F32)
        hist1 = jnp.concatenate([st[:, 1:2]] + [zero] * (t_new - 1), axis=1).reshape(ns_rows, CONV_CH)
        hist2 = jnp.concatenate([st[:, 0:1], st[:, 1:2]] + [zero] * (t_new - 2), axis=1).reshape(ns_rows, CONV_CH)
        x1_s, ti_s, tg_s, u_s = _post_block(
            hs, o_s.reshape(ns_rows, ATTN_WIDTH), weights, vecs, tm=ns_rows, hist=(hist1, hist2), t_short=t_new,
            alpha=alpha, lam_init=lam_init)
        outs["ks"].append(kfs.reshape(bd, t_new, N_HEADS, 2, HEAD_DIM))
        outs["vs"].append(vfs.reshape(bd, t_new, N_HEADS, V_DIM))
        outs["cs"].append(u_s.reshape(bd, t_new, CONV_CH)[:, t_new - (CONV_K - 1):])

        top_i = jnp.concatenate([ti_p[:, :TOP_K], ti_s[:, :TOP_K]], axis=0)
        slot, tile_e, n_used, n_slots = _route(top_i, FFN_TILE)
        slot_p, slot_s = slot[:np_rows * TOP_K], slot[np_rows * TOP_K:]
        xs = _dispatch_rows(x1_p, slot_p, jnp.zeros((n_slots, d), F32), 1024)
        xs = _dispatch_rows(x1_s, slot_s, xs, ns_rows)
        w_gate, w_up = _split_gate_up(w_gate_up[l], 512)
        bgu = b_gate_up[l]
        yb = _expert_ffn(
            xs, tile_e, n_used, w_gate, w_up,
            bgu[:, None, 0::2], bgu[:, None, 1::2], w_down[l].astype(BF16), b_down[l][:, None, :], FFN_TILE)
        g2, b2 = ln2_g[l].reshape(1, d), ln2_b[l].reshape(1, d)
        hp = _combine(x1_p, tg_p, slot_p, yb, g2, b2, 256, alpha)
        hs = _combine(x1_s, tg_s, slot_s, yb, g2, b2, ns_rows, alpha)

    return (hp.reshape(bp, s, d), hs.reshape(bd, t_new, d),
            jnp.stack(outs["kp"]), jnp.stack(outs["vp"]), jnp.stack(outs["cp"]),
            jnp.stack(outs["ks"]), jnp.stack(outs["vs"]), jnp.stack(outs["cs"]))
```

```python
import functools
import math

import jax
import jax.numpy as jnp
from jax import lax
from jax.experimental import pallas as pl
from jax.experimental.pallas import tpu as pltpu

F32 = jnp.float32
BF16 = jnp.bfloat16

D_MODEL = 1024
N_HEADS = 8
HEAD_DIM = 64
V_DIM = 2 * HEAD_DIM
QK_WIDTH = N_HEADS * 2 * HEAD_DIM
ATTN_WIDTH = N_HEADS * V_DIM
ROT_DIM = HEAD_DIM // 4
ROPE_THETA = 500000.0
CONV_CH = D_MODEL // 2
CONV_K = 3
N_EXPERTS = 32
TOP_K = 4
D_FF = D_MODEL
SWIGLU_LIMIT = 7.0
SWIGLU_ALPHA = 1.702
LN_EPS = 1e-5
PAGE_SIZE = 128
LANES = 128
MIB = 1024 * 1024

QKV_COLS = 2 * QK_WIDTH + ATTN_WIDTH
REST_COLS = 3 * CONV_CH + 2 * D_MODEL

PAGES_PER_STEP = 8
FFN_TILE = 512


def _params(semantics, vmem_mib):
    return pltpu.CompilerParams(dimension_semantics=semantics, vmem_limit_bytes=vmem_mib * MIB)


def _dot(a, b):
    return jnp.dot(a, b, preferred_element_type=F32)


def _dot_nt(a, b):
    return lax.dot_general(a, b, (((1,), (1,)), ((), ())), preferred_element_type=F32)


def _rope_tables(pos):
    inv = ROPE_THETA ** (-jnp.arange(0, ROT_DIM, 2, dtype=F32) / ROT_DIM)
    ang = pos.astype(F32)[:, None] * inv[None, :]
    cos, sin = jnp.cos(ang), jnp.sin(ang)
    n = pos.shape[0]
    half = ROT_DIM // 2
    rest = HEAD_DIM - ROT_DIM
    c = jnp.concatenate([cos, cos, jnp.ones((n, rest), F32)], axis=1)
    s_up = jnp.concatenate([-sin, jnp.zeros((n, half + rest), F32)], axis=1)
    s_dn = jnp.concatenate([jnp.zeros((n, half), F32), sin, jnp.zeros((n, rest), F32)], axis=1)
    rep = LANES // HEAD_DIM
    return tuple(jnp.tile(t, (1, rep)) for t in (c, s_up, s_dn))


def _qkv_kernel(x_ref, w_ref, c_ref, su_ref, sd_ref, q1_ref, q2_ref, kf_ref, kb_ref, vf_ref, vt_ref):
    x = x_ref[...].astype(BF16)
    c, s_up, s_dn = c_ref[...], su_ref[...], sd_ref[...]
    half = ROT_DIM // 2
    first_map = lax.broadcasted_iota(jnp.int32, c.shape, 1) < HEAD_DIM
    scale = HEAD_DIM ** -0.5

    def rope(blk):
        return blk * c + pltpu.roll(blk, LANES - half, 1) * s_up + pltpu.roll(blk, half, 1) * s_dn

    hq = _dot(x, w_ref[:, 0:QK_WIDTH])
    for j in range(QK_WIDTH // LANES):
        sl = slice(LANES * j, LANES * (j + 1))
        r = rope(hq[:, sl]) * scale
        q1_ref[:, sl] = jnp.where(first_map, r, 0.0).astype(BF16)
        q2_ref[:, sl] = jnp.where(first_map, 0.0, r).astype(BF16)
    hk = _dot(x, w_ref[:, QK_WIDTH:2 * QK_WIDTH])
    for j in range(QK_WIDTH // LANES):
        sl = slice(LANES * j, LANES * (j + 1))
        r = rope(hk[:, sl])
        kf_ref[:, sl] = r
        kb_ref[:, sl] = r.astype(BF16)
    hv = _dot(x, w_ref[:, 2 * QK_WIDTH:QKV_COLS])
    vf_ref[...] = hv
    vt_ref[...] = hv.T.astype(BF16)


def _qkv_proj(x2d, w_qkv, tables, tm):
    n = x2d.shape[0]
    seq = tables[0].shape[0]
    n_pos_blocks = seq // tm
    row = lambda i: (i, 0)
    tab = pl.BlockSpec((tm, LANES), lambda i: (i % n_pos_blocks, 0))
    wide = lambda: pl.BlockSpec((tm, QK_WIDTH), row)
    shp = lambda dt: jax.ShapeDtypeStruct((n, QK_WIDTH), dt)
    vt_spec = pl.BlockSpec((None, ATTN_WIDTH, tm), lambda i: (i // n_pos_blocks, 0, i % n_pos_blocks))
    return pl.pallas_call(
        _qkv_kernel,
        grid=(n // tm,),
        in_specs=[pl.BlockSpec((tm, D_MODEL), row),
                  pl.BlockSpec((D_MODEL, QKV_COLS), lambda i: (0, 0)),
                  tab, tab, tab],
        out_specs=[wide() for _ in range(5)] + [vt_spec],
        out_shape=[shp(BF16), shp(BF16), shp(F32), shp(BF16), shp(F32),
                   jax.ShapeDtypeStruct((n // seq, ATTN_WIDTH, seq), BF16)],
        compiler_params=_params(("parallel",), 48),
        name="qkv_proj",
    )(x2d, w_qkv, *tables)


def _lambda(lp, lam_init):
    a = jnp.sum(lp[0:1] * lp[1:2], axis=-1, keepdims=True)
    b = jnp.sum(lp[2:3] * lp[3:4], axis=-1, keepdims=True)
    return jnp.exp(a) - jnp.exp(b) + lam_init


FLASH_GROUP = 256


def _flash_kernel(lp_ref, q1_ref, q2_ref, k_ref, vt_ref, o_ref, m_sc, l_sc, acc_sc, sa_sc, sb_sc, *, tq, lam_init):
    qi = pl.program_id(2)
    m_sc[...] = jnp.full(m_sc.shape, -jnp.inf, F32)
    l_sc[...] = jnp.zeros(l_sc.shape, F32)
    acc_sc[...] = jnp.zeros(acc_sc.shape, F32)
    gw = FLASH_GROUP

    def scores(ki, s_sc):
        k = k_ref[pl.ds(pl.multiple_of(ki * tq, tq), tq), :]
        s_sc[:, 0:tq] = _dot_nt(k, q1_ref[...])
        s_sc[:, tq:2 * tq] = _dot_nt(k, q2_ref[...])

    def consume(ki, s_sc, masked):
        vt = vt_ref[:, pl.ds(pl.multiple_of(ki * tq, tq), tq)]
        for m in range(2):
            probs, alphas = [], []
            for c0 in range(0, tq, gw):
                cols = slice(m * tq + c0, m * tq + c0 + gw)
                s = s_sc[:, cols]
                if masked:
                    key = lax.broadcasted_iota(jnp.int32, s.shape, 0)
                    qry = lax.broadcasted_iota(jnp.int32, s.shape, 1) + c0
                    s = jnp.where(key <= qry, s, -jnp.inf)
                m_prev = m_sc[:, cols]
                m_new = jnp.maximum(m_prev, jnp.max(s, axis=0, keepdims=True))
                alpha = jnp.exp(m_prev - m_new)
                p = jnp.exp(s - m_new)
                l_sc[:, cols] = alpha * l_sc[:, cols] + jnp.sum(p, axis=0, keepdims=True)
                m_sc[:, cols] = m_new
                probs.append(p.astype(BF16))
                alphas.append(alpha)
            cols = slice(m * tq, (m + 1) * tq)
            acc_sc[:, cols] = (jnp.concatenate(alphas, axis=1) * acc_sc[:, cols]
                               + _dot(vt, jnp.concatenate(probs, axis=1)))

    scores(0, sa_sc)

    def pair(j, carry):
        scores(2 * j + 1, sb_sc)
        consume(2 * j, sa_sc, False)
        scores(2 * j + 2, sa_sc)
        consume(2 * j + 1, sb_sc, False)
        return carry

    lax.fori_loop(0, qi // 2, pair, 0)

    @pl.when(qi % 2 == 0)
    def _():
        consume(qi, sa_sc, True)

    @pl.when(qi % 2 == 1)
    def _():
        scores(qi, sb_sc)
        consume(qi - 1, sa_sc, False)
        consume(qi, sb_sc, True)

    o = acc_sc[...] / l_sc[...]
    lam = _lambda(lp_ref[...], lam_init)
    o_ref[...] = (o[:, :tq] - lam * o[:, tq:]).T


def _flash_attention(lp, q1, q2, kb, vt, lam_init, tq):
    b, s, _ = q1.shape
    qspec = pl.BlockSpec((None, tq, LANES), lambda bi, h, qi: (bi, qi, h))
    kspec = pl.BlockSpec((None, s, LANES), lambda bi, h, qi: (bi, 0, h))
    vspec = pl.BlockSpec((None, V_DIM, s), lambda bi, h, qi: (bi, h, 0))
    return pl.pallas_call(
        functools.partial(_flash_kernel, tq=tq, lam_init=lam_init),
        grid=(b, N_HEADS, s // tq),
        in_specs=[pl.BlockSpec((4, HEAD_DIM), lambda bi, h, qi: (0, 0)), qspec, qspec, kspec, vspec],
        out_specs=qspec,
        out_shape=jax.ShapeDtypeStruct((b, s, ATTN_WIDTH), F32),
        scratch_shapes=[pltpu.VMEM((1, 2 * tq), F32), pltpu.VMEM((1, 2 * tq), F32),
                        pltpu.VMEM((V_DIM, 2 * tq), F32),
                        pltpu.VMEM((tq, 2 * tq), F32), pltpu.VMEM((tq, 2 * tq), F32)],
        compiler_params=_params(("parallel", "parallel", "arbitrary"), 48),
        name="flash_diff_attn",
    )(lp, q1, q2, kb, vt)


def _paged_kernel(pt_ref, lp_ref, q1_ref, q2_ref, kn_ref, vn_ref, *rest, t_new, lam_init):
    np_ = PAGES_PER_STEP
    k_pages, v_pages = rest[:np_], rest[np_:2 * np_]
    o_ref, m_sc, l_sc, acc_sc, kc_sc, vc_sc = rest[2 * np_:]
    j = pl.program_id(1)
    rows = 2 * t_new * N_HEADS

    @pl.when(j == 0)
    def _():
        m_sc[...] = jnp.full(m_sc.shape, -jnp.inf, F32)
        l_sc[...] = jnp.zeros(l_sc.shape, F32)
        acc_sc[...] = jnp.zeros(acc_sc.shape, F32)

    head_of_lane = lax.broadcasted_iota(jnp.int32, (N_HEADS, QK_WIDTH), 1) // LANES
    own_head = head_of_lane == lax.broadcasted_iota(jnp.int32, (N_HEADS, QK_WIDTH), 0)
    pieces = []
    for q_ref in (q1_ref, q2_ref):
        for t in range(t_new):
            qt = jnp.broadcast_to(q_ref[t:t + 1, :], (N_HEADS, QK_WIDTH))
            pieces.append(jnp.where(own_head, qt, 0.0))
    qbd = jnp.concatenate(pieces, axis=0).astype(BF16)

    def update(s, v):
        m_prev = m_sc[...]
        m_new = jnp.maximum(m_prev, jnp.max(s, axis=-1, keepdims=True))
        alpha = jnp.exp(m_prev - m_new)
        p = jnp.exp(s - m_new)
        l_sc[...] = alpha * l_sc[...] + jnp.sum(p, axis=-1, keepdims=True)
        acc_sc[...] = alpha * acc_sc[...] + _dot(p.astype(BF16), v)
        m_sc[...] = m_new

    for i in range(np_):
        kc_sc[:, PAGE_SIZE * i:PAGE_SIZE * (i + 1)] = k_pages[i][...].astype(BF16)
        for h in range(N_HEADS):
            vh = v_pages[i][pl.ds(h, PAGE_SIZE, stride=N_HEADS), :]
            vc_sc[PAGE_SIZE * i:PAGE_SIZE * (i + 1), V_DIM * h:V_DIM * (h + 1)] = vh.astype(BF16)
    update(_dot(qbd, kc_sc[...]), vc_sc[...])

    @pl.when(j == pl.num_programs(1) - 1)
    def _():
        s = _dot_nt(qbd, kn_ref[...].astype(BF16))
        r = lax.broadcasted_iota(jnp.int32, s.shape, 0)
        tok = (r // N_HEADS) % t_new
        col = lax.broadcasted_iota(jnp.int32, s.shape, 1)
        s = jnp.where(col <= tok, s, -jnp.inf)
        update(s, vn_ref[...].astype(BF16))
        o = acc_sc[...] / l_sc[...]
        lam = _lambda(lp_ref[...], lam_init)
        half = rows // 2
        d = o[:half] - lam * o[half:]
        for t in range(t_new):
            blk = jnp.where(own_head, d[N_HEADS * t:N_HEADS * (t + 1)], 0.0)
            o_ref[t:t + 1, :] = jnp.sum(blk, axis=0, keepdims=True)


def _paged_attention(lp, page_table, q1, q2, k_new, v_new, cache_k2, cache_v2, lam_init):
    bd, t_new, _ = q1.shape
    n_pages = page_table.shape[1]
    np_ = PAGES_PER_STEP
    rows = 2 * t_new * N_HEADS
    chunk = np_ * PAGE_SIZE
    seq = lambda b, j, pt: (b, 0, 0)
    qspec = pl.BlockSpec((None, t_new, QK_WIDTH), seq)
    nspec = pl.BlockSpec((None, PAGE_SIZE, QK_WIDTH), seq)

    def k_page_spec(i):
        return pl.BlockSpec((QK_WIDTH, PAGE_SIZE), lambda b, j, pt: (pt[b, j * np_ + i], 0))

    def v_page_spec(i):
        return pl.BlockSpec((PAGE_SIZE * N_HEADS, V_DIM), lambda b, j, pt: (pt[b, j * np_ + i], 0))

    grid_spec = pltpu.PrefetchScalarGridSpec(
        num_scalar_prefetch=1,
        grid=(bd, n_pages // np_),
        in_specs=([pl.BlockSpec((4, HEAD_DIM), lambda b, j, pt: (0, 0)), qspec, qspec, nspec, nspec]
                  + [k_page_spec(i) for i in range(np_)] + [v_page_spec(i) for i in range(np_)]),
        out_specs=qspec,
        scratch_shapes=[pltpu.VMEM((rows, 1), F32), pltpu.VMEM((rows, 1), F32),
                        pltpu.VMEM((rows, ATTN_WIDTH), F32),
                        pltpu.VMEM((QK_WIDTH, chunk), BF16), pltpu.VMEM((chunk, ATTN_WIDTH), BF16)],
    )
    return pl.pallas_call(
        functools.partial(_paged_kernel, t_new=t_new, lam_init=lam_init),
        grid_spec=grid_spec,
        out_shape=jax.ShapeDtypeStruct((bd, t_new, ATTN_WIDTH), F32),
        compiler_params=_params(("parallel", "arbitrary"), 48),
        name="paged_diff_attn",
    )(page_table, lp, q1, q2, k_new, v_new, *([cache_k2] * np_), *([cache_v2] * np_))


def _post_kernel(*refs, tm, tiles_per_seq, t_short, alpha, lam_init):
    long_seq = tiles_per_seq > 0
    (x_ref, o_ref, w2_ref, wa_ref, wc_ref, wo_ref, wr_ref, g_ref, cw_ref, cb_ref,
     l1g_ref, l1b_ref, br_ref) = refs[:13]
    if long_seq:
        st_ref, x1_ref, ti_ref, tg_ref, nc_ref, carry = refs[13:]
    else:
        h1_ref, h2_ref, x1_ref, ti_ref, tg_ref, u_ref = refs[13:]

    x = x_ref[...]
    h = _dot(x.astype(BF16), w2_ref[...])
    c0 = CONV_CH
    bg, cg, xc = h[:, 0:c0], h[:, c0:2 * c0], h[:, 2 * c0:3 * c0]
    ga = h[:, 3 * c0:3 * c0 + D_MODEL]
    gc = h[:, 3 * c0 + D_MODEL:]

    u = cg * xc
    row = lax.broadcasted_iota(jnp.int32, u.shape, 0)
    if long_seq:
        first = pl.program_id(0) % tiles_per_seq == 0
        st = st_ref[...]
        prev2 = jnp.where(first, st[0:1], carry[0:1])
        prev1 = jnp.where(first, st[1:2], carry[1:2])
        u1 = jnp.where(row == 0, prev1, pltpu.roll(u, 1, 0))
        u2 = jnp.where(row == 0, prev2, jnp.where(row == 1, prev1, pltpu.roll(u, 2, 0)))
        carry[0:2, :] = u[tm - 2:tm]
        nc_ref[...] = u[tm - 2:tm]
    else:
        tpos = row % t_short
        u1 = jnp.where(tpos >= 1, pltpu.roll(u, 1, 0), h1_ref[...])
        u2 = jnp.where(tpos >= 2, pltpu.roll(u, 2, 0), h2_ref[...])
        u_ref[...] = u
    cw = cw_ref[...]
    y_c = bg * (cb_ref[...] + u2 * cw[0:1] + u1 * cw[1:2] + u * cw[2:3])
    y_c = _dot(y_c.astype(BF16), wc_ref[...])

    o = o_ref[...]
    gain = g_ref[...]
    heads = []
    for hd in range(N_HEADS):
        oh = o[:, V_DIM * hd:V_DIM * (hd + 1)]
        ms = jnp.mean(oh * oh, axis=-1, keepdims=True)
        heads.append(oh * lax.rsqrt(ms + LN_EPS) * gain * (1.0 - lam_init))
    y_a = _dot(jnp.concatenate(heads, axis=1).astype(BF16), wa_ref[...])

    mix = jax.nn.sigmoid(ga) * y_a + jax.nn.sigmoid(gc) * y_c
    z = alpha * x + _dot(mix.astype(BF16), wo_ref[...])
    mu = jnp.mean(z, axis=-1, keepdims=True)
    zc = z - mu
    var = jnp.mean(zc * zc, axis=-1, keepdims=True)
    x1 = zc * lax.rsqrt(var + LN_EPS) * l1g_ref[...] + l1b_ref[...]
    x1_ref[...] = x1

    logits = _dot(x1.astype(BF16), wr_ref[...]) + br_ref[...]
    lane = lax.broadcasted_iota(jnp.int32, logits.shape, 1)
    logits = jnp.where(lane < N_EXPERTS, logits, -jnp.inf)
    idx_out = jnp.zeros(logits.shape, jnp.int32)
    val_out = jnp.zeros(logits.shape, F32)
    top0 = None
    denom = jnp.zeros((tm, 1), F32)
    for k in range(TOP_K):
        mx = jnp.max(logits, axis=-1, keepdims=True)
        idx = jnp.min(jnp.where(logits == mx, lane, LANES), axis=-1, keepdims=True)
        if k == 0:
            top0 = mx
        e = jnp.exp(mx - top0)
        denom = denom + e
        idx_out = jnp.where(lane == k, idx, idx_out)
        val_out = jnp.where(lane == k, e, val_out)
        logits = jnp.where(lane == idx, -jnp.inf, logits)
    ti_ref[...] = idx_out
    tg_ref[...] = val_out / denom


def _post_block(x2d, o2d, weights, vecs, *, tm, tiles_per_seq=0, conv_state=None, hist=None, t_short=0,
                alpha, lam_init):
    n = x2d.shape[0]
    row = lambda i: (i, 0)
    full = lambda a: pl.BlockSpec(a.shape, lambda i: (0,) * a.ndim)
    in_specs = [pl.BlockSpec((tm, D_MODEL), row), pl.BlockSpec((tm, ATTN_WIDTH), row)]
    in_specs += [full(w) for w in weights] + [full(v) for v in vecs]
    out_specs = [pl.BlockSpec((tm, D_MODEL), row), pl.BlockSpec((tm, LANES), row), pl.BlockSpec((tm, LANES), row)]
    out_shape = [jax.ShapeDtypeStruct((n, D_MODEL), F32), jax.ShapeDtypeStruct((n, LANES), jnp.int32),
                 jax.ShapeDtypeStruct((n, LANES), F32)]
    scratch = []
    if tiles_per_seq > 0:
        extra = [conv_state]
        in_specs.append(pl.BlockSpec((None, CONV_K - 1, CONV_CH), lambda i: (i // tiles_per_seq, 0, 0)))
        out_specs.append(pl.BlockSpec((None, CONV_K - 1, CONV_CH), lambda i: (i // tiles_per_seq, 0, 0)))
        out_shape.append(jax.ShapeDtypeStruct(conv_state.shape, F32))
        scratch.append(pltpu.VMEM((8, CONV_CH), F32))
    else:
        extra = list(hist)
        in_specs += [pl.BlockSpec((tm, CONV_CH), row)] * 2
        out_specs.append(pl.BlockSpec((tm, CONV_CH), row))
        out_shape.append(jax.ShapeDtypeStruct((n, CONV_CH), F32))
    return pl.pallas_call(
        functools.partial(_post_kernel, tm=tm, tiles_per_seq=tiles_per_seq, t_short=t_short, alpha=alpha,
                          lam_init=lam_init),
        grid=(n // tm,),
        in_specs=in_specs,
        out_specs=out_specs,
        out_shape=out_shape,
        scratch_shapes=scratch,
        compiler_params=_params(("arbitrary",), 56),
        name="post_block",
    )(x2d, o2d, *weights, *vecs, *extra)


def _route(top_i, tm):
    n = top_i.shape[0]
    a = n * TOP_K
    flat_e = top_i.reshape(a)
    onehot = (flat_e[:, None] == jnp.arange(N_EXPERTS, dtype=jnp.int32)[None, :]).astype(jnp.int32)
    csum = jnp.cumsum(onehot, axis=0)
    rank = jnp.take_along_axis(csum, flat_e[:, None], axis=1)[:, 0] - 1
    counts = csum[-1]
    padded = (counts + tm - 1) // tm * tm
    pad_end = jnp.cumsum(padded)
    pad_start = pad_end - padded
    slot = (pad_start[flat_e] + rank).astype(jnp.int32)
    n_tiles = (a + N_EXPERTS * (tm - 1) + tm - 1) // tm
    tile_start = jnp.arange(n_tiles, dtype=jnp.int32) * tm
    tile_e = jnp.minimum(jnp.sum((pad_end[None, :] <= tile_start[:, None]).astype(jnp.int32), axis=1),
                         N_EXPERTS - 1).astype(jnp.int32)
    n_used = (pad_end[-1] // tm).astype(jnp.int32).reshape(1)
    return slot, tile_e, n_used, n_tiles * tm


def _dispatch_kernel(slot_ref, x_ref, xs_in_hbm, xs_hbm, sem, *, rows):
    del xs_in_hbm

    def issue(r, c):
        for k in range(TOP_K):
            pltpu.make_async_copy(x_ref.at[pl.ds(r, 1), :],
                                  xs_hbm.at[pl.ds(slot_ref[r * TOP_K + k], 1), :], sem).start()
        return c

    lax.fori_loop(0, rows, issue, 0)
    for _ in range(TOP_K):
        pltpu.make_async_copy(x_ref, xs_hbm.at[pl.ds(0, rows), :], sem).wait()


def _dispatch_rows(x1, slot, xs, rows):
    n = x1.shape[0]
    return pl.pallas_call(
        functools.partial(_dispatch_kernel, rows=rows),
        grid=(n // rows,),
        in_specs=[pl.BlockSpec((rows * TOP_K,), lambda i: (i,), memory_space=pltpu.SMEM),
                  pl.BlockSpec((rows, D_MODEL), lambda i: (i, 0)), pl.BlockSpec(memory_space=pl.ANY)],
        out_specs=pl.BlockSpec(memory_space=pl.ANY),
        out_shape=jax.ShapeDtypeStruct(xs.shape, xs.dtype),
        scratch_shapes=[pltpu.SemaphoreType.DMA(())],
        input_output_aliases={2: 0},
        compiler_params=_params(("arbitrary",), 16),
        name="moe_dispatch",
    )(slot, x1, xs)


def _split_kernel(w_ref, p_ref, g_ref, u_ref):
    w = w_ref[...].astype(BF16)
    half = 2 * LANES
    for t in range(D_FF // half):
        y = _dot(w[:, 2 * half * t:2 * half * (t + 1)], p_ref[...])
        g_ref[:, half * t:half * (t + 1)] = y[:, :half].astype(BF16)
        u_ref[:, half * t:half * (t + 1)] = y[:, half:].astype(BF16)


def _split_gate_up(w_gate_up, rows):
    e, dm, f2 = w_gate_up.shape
    half = 2 * LANES
    r = lax.broadcasted_iota(jnp.int32, (2 * half, 2 * half), 0)
    c = lax.broadcasted_iota(jnp.int32, (2 * half, 2 * half), 1)
    perm = jnp.where(c < half, r == 2 * c, r == 2 * (c - half) + 1).astype(BF16)
    out = pl.BlockSpec((None, rows, f2 // 2), lambda ei, i: (ei, i, 0))
    return pl.pallas_call(
        _split_kernel,
        grid=(e, dm // rows),
        in_specs=[pl.BlockSpec((None, rows, f2), lambda ei, i: (ei, i, 0)),
                  pl.BlockSpec((2 * half, 2 * half), lambda ei, i: (0, 0))],
        out_specs=[out, out],
        out_shape=[jax.ShapeDtypeStruct((e, dm, f2 // 2), BF16)] * 2,
        compiler_params=_params(("parallel", "parallel"), 32),
        name="split_gate_up",
    )(w_gate_up, perm)


def _ffn_kernel(te_ref, nu_ref, xs_ref, wg_ref, wu_ref, bg_ref, bu_ref, wd_ref, bd_ref, o_ref):
    i = pl.program_id(0)

    @pl.when(i < nu_ref[0])
    def _():
        x = xs_ref[...].astype(BF16)
        g = jnp.minimum(_dot(x, wg_ref[...]) + bg_ref[...], SWIGLU_LIMIT)
        u = jnp.clip(_dot(x, wu_ref[...]) + bu_ref[...], -SWIGLU_LIMIT, SWIGLU_LIMIT)
        act = (u + 1.0) * g * jax.nn.sigmoid(SWIGLU_ALPHA * g)
        o_ref[...] = _dot(act.astype(BF16), wd_ref[...]) + bd_ref[...]

    @pl.when(i >= nu_ref[0])
    def _():
        o_ref[...] = jnp.zeros(o_ref.shape, F32)


def _expert_ffn(xs, tile_e, n_used, wg, wu, bg, bu, wd, bd, tm):
    n_slots = xs.shape[0]
    row = lambda i, te, nu: (i, 0)
    wsel = lambda i, te, nu: (te[i], 0, 0)
    grid_spec = pltpu.PrefetchScalarGridSpec(
        num_scalar_prefetch=2,
        grid=(n_slots // tm,),
        in_specs=[pl.BlockSpec((tm, D_MODEL), row),
                  pl.BlockSpec((None, D_MODEL, D_FF), wsel), pl.BlockSpec((None, D_MODEL, D_FF), wsel),
                  pl.BlockSpec((None, 1, D_FF), wsel), pl.BlockSpec((None, 1, D_FF), wsel),
                  pl.BlockSpec((None, D_FF, D_MODEL), wsel), pl.BlockSpec((None, 1, D_MODEL), wsel)],
        out_specs=pl.BlockSpec((tm, D_MODEL), row),
    )
    return pl.pallas_call(
        _ffn_kernel,
        grid_spec=grid_spec,
        out_shape=jax.ShapeDtypeStruct((n_slots, D_MODEL), F32),
        compiler_params=_params(("arbitrary",), 48),
        name="moe_ffn",
    )(tile_e, n_used, xs, wg, wu, bg, bu, wd, bd)


def _combine_kernel(slot_ref, x1_ref, gate_ref, y_hbm, g_ref, b_ref, o_ref, buf, sem, *, rows, alpha):
    def issue(r, c):
        for k in range(TOP_K):
            pltpu.make_async_copy(y_hbm.at[pl.ds(slot_ref[r * TOP_K + k], 1), :],
                                  buf.at[k, pl.ds(r, 1), :], sem).start()
        return c

    lax.fori_loop(0, rows, issue, 0)
    for k in range(TOP_K):
        pltpu.make_async_copy(y_hbm.at[pl.ds(0, rows), :], buf.at[k], sem).wait()
    gate = gate_ref[...]
    f = ((gate[:, 0:1] * buf[0] + gate[:, 1:2] * buf[1]) + (gate[:, 2:3] * buf[2] + gate[:, 3:4] * buf[3]))
    z = alpha * x1_ref[...] + f
    mu = jnp.mean(z, axis=-1, keepdims=True)
    zc = z - mu
    var = jnp.mean(zc * zc, axis=-1, keepdims=True)
    o_ref[...] = zc * lax.rsqrt(var + LN_EPS) * g_ref[...] + b_ref[...]


def _combine(x1, gate, slot, yb, ln_g, ln_b, rows, alpha):
    n = x1.shape[0]
    row = lambda i: (i, 0)
    vec = pl.BlockSpec((1, D_MODEL), lambda i: (0, 0))
    return pl.pallas_call(
        functools.partial(_combine_kernel, rows=rows, alpha=alpha),
        grid=(n // rows,),
        in_specs=[pl.BlockSpec((rows * TOP_K,), lambda i: (i,), memory_space=pltpu.SMEM),
                  pl.BlockSpec((rows, D_MODEL), row), pl.BlockSpec((rows, LANES), row),
                  pl.BlockSpec(memory_space=pl.ANY), vec, vec],
        out_specs=pl.BlockSpec((rows, D_MODEL), row),
        out_shape=jax.ShapeDtypeStruct((n, D_MODEL), F32),
        scratch_shapes=[pltpu.VMEM((TOP_K, rows, D_MODEL), F32), pltpu.SemaphoreType.DMA(())],
        compiler_params=_params(("arbitrary",), 32),
        name="moe_combine",
    )(slot, x1, gate, yb, ln_g, ln_b)


def kernel(x_prompt, x_sample, cache_k, cache_v, state_conv, page_table, w_in, lambda_q1, lambda_k1,
           lambda_q2, lambda_k2, subln_g, w_attn_out, w_conv, b_conv, w_conv_out, w_o, ln1_g, ln1_b,
           w_router, b_router, w_gate_up, b_gate_up, w_down, b_down, ln2_g, ln2_b):
    bp, s, d = x_prompt.shape
    bd, t_new, _ = x_sample.shape
    depth = w_in.shape[0]
    n_phys = cache_k.shape[1]
    past = page_table.shape[1] * PAGE_SIZE
    alpha = (2.0 * depth) ** 0.25
    np_rows, ns_rows = bp * s, bd * t_new
    tm_p = 512
    tq = 512

    tab_p = _rope_tables(jnp.arange(s))
    tab_s = _rope_tables(jnp.tile(past + jnp.arange(t_new), bd))

    cache_k2 = jnp.transpose(cache_k, (0, 1, 3, 4, 5, 2)).reshape(depth * n_phys * QK_WIDTH, PAGE_SIZE)
    cache_v2 = cache_v.reshape(depth * n_phys * PAGE_SIZE * N_HEADS, V_DIM)
    hp = x_prompt.reshape(np_rows, d)
    hs = x_sample.reshape(ns_rows, d)
    outs = {k: [] for k in ("kp", "vp", "cp", "ks", "vs", "cs")}
    for l in range(depth):
        lam_init = 0.8 - 0.6 * math.exp(-0.3 * l)
        lp = jnp.concatenate([lambda_q1[l:l + 1], lambda_k1[l:l + 1], lambda_q2[l:l + 1], lambda_k2[l:l + 1]], axis=0)
        w_l = w_in[l]
        w_qkv = w_l[:, :QKV_COLS].astype(BF16)
        w_rest = w_l[:, QKV_COLS:].astype(BF16)
        w_r = jnp.pad(w_router[l], ((0, 0), (0, LANES - N_EXPERTS))).astype(BF16)
        b_r = jnp.pad(b_router[l], (0, LANES - N_EXPERTS)).reshape(1, LANES)
        weights = (w_rest, w_attn_out[l].astype(BF16), w_conv_out[l].astype(BF16), w_o[l].astype(BF16), w_r)
        vecs = (subln_g[l].reshape(1, V_DIM), w_conv[l],
                b_conv[l].reshape(1, CONV_CH), ln1_g[l].reshape(1, d), ln1_b[l].reshape(1, d), b_r)

        q1, q2, kf, kb, vf, vt = _qkv_proj(hp, w_qkv, tab_p, tm_p)
        shp = (bp, s, QK_WIDTH)
        o_p = _flash_attention(lp, q1.reshape(shp), q2.reshape(shp), kb.reshape(shp), vt, lam_init, tq)
        x1_p, ti_p, tg_p, nc_p = _post_block(
            hp, o_p.reshape(np_rows, ATTN_WIDTH), weights, vecs, tm=tm_p // 2, tiles_per_seq=s // (tm_p // 2),
            conv_state=jnp.zeros((bp, CONV_K - 1, CONV_CH), F32), alpha=alpha, lam_init=lam_init)
        outs["kp"].append(kf.reshape(bp, s, N_HEADS, 2, HEAD_DIM))
        outs["vp"].append(vf.reshape(bp, s, N_HEADS, V_DIM))
        outs["cp"].append(nc_p)

        q1s, q2s, kfs, _, vfs, _ = _qkv_proj(hs, w_qkv, tab_s, ns_rows)
        seq3 = lambda a: a.reshape(bd, t_new, QK_WIDTH)
        pad_page = lambda a: jnp.pad(seq3(a), ((0, 0), (0, PAGE_SIZE - t_new), (0, 0)))
        o_s = _paged_attention(
            lp, page_table + l * n_phys, seq3(q1s).astype(F32), seq3(q2s).astype(F32), pad_page(kfs), pad_page(vfs),
            cache_k2, cache_v2, lam_init)
        st = state_conv[l]
        zero = jnp.zeros((bd, 1, CONV_CH), F32)
        hist1 = jnp.concatenate([st[:, 1:2]] + [zero] * (t_new - 1), axis=1).reshape(ns_rows, CONV_CH)
        hist2 = jnp.concatenate([st[:, 0:1], st[:, 1:2]] + [zero] * (t_new - 2), axis=1).reshape(ns_rows, CONV_CH)
        x1_s, ti_s, tg_s, u_s = _post_block(
            hs, o_s.reshape(ns_rows, ATTN_WIDTH), weights, vecs, tm=ns_rows, hist=(hist1, hist2), t_short=t_new,
            alpha=alpha, lam_init=lam_init)
        outs["ks"].append(kfs.reshape(bd, t_new, N_HEADS, 2, HEAD_DIM))
        outs["vs"].append(vfs.reshape(bd, t_new, N_HEADS, V_DIM))
        outs["cs"].append(u_s.reshape(bd, t_new, CONV_CH)[:, t_new - (CONV_K - 1):])

        top_i = jnp.concatenate([ti_p[:, :TOP_K], ti_s[:, :TOP_K]], axis=0)
        slot, tile_e, n_used, n_slots = _route(top_i, FFN_TILE)
        slot_p, slot_s = slot[:np_rows * TOP_K], slot[np_rows * TOP_K:]
        xs = _dispatch_rows(x1_p, slot_p, jnp.zeros((n_slots, d), F32), 1024)
        xs = _dispatch_rows(x1_s, slot_s, xs, ns_rows)
        w_gate, w_up = _split_gate_up(w_gate_up[l], 512)
        bgu = b_gate_up[l]
        yb = _expert_ffn(
            xs, tile_e, n_used, w_gate, w_up,
            bgu[:, None, 0::2], bgu[:, None, 1::2], w_down[l].astype(BF16), b_down[l][:, None, :], FFN_TILE)
        g2, b2 = ln2_g[l].reshape(1, d), ln2_b[l].reshape(1, d)
        hp = _combine(x1_p, tg_p, slot_p, yb, g2, b2, 256, alpha)
        hs = _combine(x1_s, tg_s, slot_s, yb, g2, b2, ns_rows, alpha)

    return (hp.reshape(bp, s, d), hs.reshape(bd, t_new, d),
            jnp.stack(outs["kp"]), jnp.stack(outs["vp"]), jnp.stack(outs["cp"]),
            jnp.stack(outs["ks"]), jnp.stack(outs["vs"]), jnp.stack(outs["cs"]))
```

```python
import functools
import math

import jax
import jax.numpy as jnp
from jax import lax
from jax.experimental import pallas as pl
from jax.experimental.pallas import tpu as pltpu

F32 = jnp.float32
BF16 = jnp.bfloat16

D_MODEL = 1024
N_HEADS = 8
HEAD_DIM = 64
V_DIM = 2 * HEAD_DIM
QK_WIDTH = N_HEADS * 2 * HEAD_DIM
ATTN_WIDTH = N_HEADS * V_DIM
ROT_DIM = HEAD_DIM // 4
ROPE_THETA = 500000.0
CONV_CH = D_MODEL // 2
CONV_K = 3
N_EXPERTS = 32
TOP_K = 4
D_FF = D_MODEL
SWIGLU_LIMIT = 7.0
SWIGLU_ALPHA = 1.702
LN_EPS = 1e-5
PAGE_SIZE = 128
LANES = 128
MIB = 1024 * 1024

QKV_COLS = 2 * QK_WIDTH + ATTN_WIDTH
REST_COLS = 3 * CONV_CH + 2 * D_MODEL

PAGES_PER_STEP = 8
FFN_TILE = 512
LOG2_E = math.log2(math.e)


def _params(semantics, vmem_mib):
    return pltpu.CompilerParams(dimension_semantics=semantics, vmem_limit_bytes=vmem_mib * MIB)


def _dot(a, b):
    return jnp.dot(a, b, preferred_element_type=F32)


def _dot_nt(a, b):
    return lax.dot_general(a, b, (((1,), (1,)), ((), ())), preferred_element_type=F32)


def _rope_tables(pos):
    inv = ROPE_THETA ** (-jnp.arange(0, ROT_DIM, 2, dtype=F32) / ROT_DIM)
    ang = pos.astype(F32)[:, None] * inv[None, :]
    cos, sin = jnp.cos(ang), jnp.sin(ang)
    n = pos.shape[0]
    half = ROT_DIM // 2
    rest = HEAD_DIM - ROT_DIM
    c = jnp.concatenate([cos, cos, jnp.ones((n, rest), F32)], axis=1)
    s_up = jnp.concatenate([-sin, jnp.zeros((n, half + rest), F32)], axis=1)
    s_dn = jnp.concatenate([jnp.zeros((n, half), F32), sin, jnp.zeros((n, rest), F32)], axis=1)
    rep = LANES // HEAD_DIM
    return tuple(jnp.tile(t, (1, rep)) for t in (c, s_up, s_dn))


def _qkv_kernel(x_ref, w_ref, c_ref, su_ref, sd_ref, q1_ref, q2_ref, kf_ref, kb_ref, vf_ref, vt_ref):
    x = x_ref[...].astype(BF16)
    c, s_up, s_dn = c_ref[...], su_ref[...], sd_ref[...]
    half = ROT_DIM // 2
    first_map = lax.broadcasted_iota(jnp.int32, c.shape, 1) < HEAD_DIM
    scale = HEAD_DIM ** -0.5 * LOG2_E

    def rope(blk):
        return blk * c + pltpu.roll(blk, LANES - half, 1) * s_up + pltpu.roll(blk, half, 1) * s_dn

    hq = _dot(x, w_ref[:, 0:QK_WIDTH])
    for j in range(QK_WIDTH // LANES):
        sl = slice(LANES * j, LANES * (j + 1))
        r = rope(hq[:, sl]) * scale
        q1_ref[:, sl] = jnp.where(first_map, r, 0.0).astype(BF16)
        q2_ref[:, sl] = jnp.where(first_map, 0.0, r).astype(BF16)
    hk = _dot(x, w_ref[:, QK_WIDTH:2 * QK_WIDTH])
    for j in range(QK_WIDTH // LANES):
        sl = slice(LANES * j, LANES * (j + 1))
        r = rope(hk[:, sl])
        kf_ref[:, sl] = r
        kb_ref[:, sl] = r.astype(BF16)
    hv = _dot(x, w_ref[:, 2 * QK_WIDTH:QKV_COLS])
    vf_ref[...] = hv
    vt_ref[...] = hv.T.astype(BF16)


def _qkv_proj(x2d, w_qkv, tables, tm):
    n = x2d.shape[0]
    seq = tables[0].shape[0]
    n_pos_blocks = seq // tm
    row = lambda i: (i, 0)
    tab = pl.BlockSpec((tm, LANES), lambda i: (i % n_pos_blocks, 0))
    wide = lambda: pl.BlockSpec((tm, QK_WIDTH), row)
    shp = lambda dt: jax.ShapeDtypeStruct((n, QK_WIDTH), dt)
    vt_spec = pl.BlockSpec((None, ATTN_WIDTH, tm), lambda i: (i // n_pos_blocks, 0, i % n_pos_blocks))
    return pl.pallas_call(
        _qkv_kernel,
        grid=(n // tm,),
        in_specs=[pl.BlockSpec((tm, D_MODEL), row),
                  pl.BlockSpec((D_MODEL, QKV_COLS), lambda i: (0, 0)),
                  tab, tab, tab],
        out_specs=[wide() for _ in range(5)] + [vt_spec],
        out_shape=[shp(BF16), shp(BF16), shp(F32), shp(BF16), shp(F32),
                   jax.ShapeDtypeStruct((n // seq, ATTN_WIDTH, seq), BF16)],
        compiler_params=_params(("parallel",), 48),
        name="qkv_proj",
    )(x2d, w_qkv, *tables)


def _lambda(lp, lam_init):
    a = jnp.sum(lp[0:1] * lp[1:2], axis=-1, keepdims=True)
    b = jnp.sum(lp[2:3] * lp[3:4], axis=-1, keepdims=True)
    return jnp.exp(a) - jnp.exp(b) + lam_init


FLASH_GROUP = 256


def _flash_kernel(lp_ref, q1_ref, q2_ref, k_ref, vt_ref, o_ref, m_sc, l_sc, acc_sc, sa_sc, sb_sc, *, tq, lam_init):
    qi = pl.program_id(2)
    m_sc[...] = jnp.full(m_sc.shape, -jnp.inf, F32)
    l_sc[...] = jnp.zeros(l_sc.shape, F32)
    acc_sc[...] = jnp.zeros(acc_sc.shape, F32)
    gw = FLASH_GROUP

    def scores(ki, s_sc):
        k = k_ref[pl.ds(pl.multiple_of(ki * tq, tq), tq), :]
        s_sc[:, 0:tq] = _dot_nt(k, q1_ref[...])
        s_sc[:, tq:2 * tq] = _dot_nt(k, q2_ref[...])

    def consume(ki, s_sc, masked):
        vt = vt_ref[:, pl.ds(pl.multiple_of(ki * tq, tq), tq)]
        for m in range(2):
            probs, alphas = [], []
            for c0 in range(0, tq, gw):
                cols = slice(m * tq + c0, m * tq + c0 + gw)
                s = s_sc[:, cols]
                if masked:
                    key = lax.broadcasted_iota(jnp.int32, s.shape, 0)
                    qry = lax.broadcasted_iota(jnp.int32, s.shape, 1) + c0
                    s = jnp.where(key <= qry, s, -jnp.inf)
                m_prev = m_sc[:, cols]
                m_new = jnp.maximum(m_prev, jnp.max(s, axis=0, keepdims=True))
                alpha = jnp.exp2(m_prev - m_new)
                p = jnp.exp2(s - m_new)
                l_sc[:, cols] = alpha * l_sc[:, cols] + jnp.sum(p, axis=0, keepdims=True)
                m_sc[:, cols] = m_new
                probs.append(p.astype(BF16))
                alphas.append(alpha)
            cols = slice(m * tq, (m + 1) * tq)
            acc_sc[:, cols] = (jnp.concatenate(alphas, axis=1) * acc_sc[:, cols]
                               + _dot(vt, jnp.concatenate(probs, axis=1)))

    scores(0, sa_sc)

    def pair(j, carry):
        scores(2 * j + 1, sb_sc)
        consume(2 * j, sa_sc, False)
        scores(2 * j + 2, sa_sc)
        consume(2 * j + 1, sb_sc, False)
        return carry

    lax.fori_loop(0, qi // 2, pair, 0)

    @pl.when(qi % 2 == 0)
    def _():
        consume(qi, sa_sc, True)

    @pl.when(qi % 2 == 1)
    def _():
        scores(qi, sb_sc)
        consume(qi - 1, sa_sc, False)
        consume(qi, sb_sc, True)

    o = acc_sc[...] / l_sc[...]
    lam = _lambda(lp_ref[...], lam_init)
    o_ref[...] = (o[:, :tq] - lam * o[:, tq:]).T


def _flash_attention(lp, q1, q2, kb, vt, lam_init, tq):
    b, s, _ = q1.shape
    qspec = pl.BlockSpec((None, tq, LANES), lambda bi, h, qi: (bi, qi, h))
    kspec = pl.BlockSpec((None, s, LANES), lambda bi, h, qi: (bi, 0, h))
    vspec = pl.BlockSpec((None, V_DIM, s), lambda bi, h, qi: (bi, h, 0))
    return pl.pallas_call(
        functools.partial(_flash_kernel, tq=tq, lam_init=lam_init),
        grid=(b, N_HEADS, s // tq),
        in_specs=[pl.BlockSpec((4, HEAD_DIM), lambda bi, h, qi: (0, 0)), qspec, qspec, kspec, vspec],
        out_specs=qspec,
        out_shape=jax.ShapeDtypeStruct((b, s, ATTN_WIDTH), F32),
        scratch_shapes=[pltpu.VMEM((1, 2 * tq), F32), pltpu.VMEM((1, 2 * tq), F32),
                        pltpu.VMEM((V_DIM, 2 * tq), F32),
                        pltpu.VMEM((tq, 2 * tq), F32), pltpu.VMEM((tq, 2 * tq), F32)],
        compiler_params=_params(("parallel", "parallel", "arbitrary"), 48),
        name="flash_diff_attn",
    )(lp, q1, q2, kb, vt)


def _paged_kernel(pt_ref, lp_ref, q1_ref, q2_ref, kn_ref, vn_ref, *rest, t_new, lam_init):
    np_ = PAGES_PER_STEP
    k_pages, v_pages = rest[:np_], rest[np_:2 * np_]
    o_ref, m_sc, l_sc, acc_sc, kc_sc, vc_sc = rest[2 * np_:]
    j = pl.program_id(1)
    rows = 2 * t_new * N_HEADS

    @pl.when(j == 0)
    def _():
        m_sc[...] = jnp.full(m_sc.shape, -jnp.inf, F32)
        l_sc[...] = jnp.zeros(l_sc.shape, F32)
        acc_sc[...] = jnp.zeros(acc_sc.shape, F32)

    head_of_lane = lax.broadcasted_iota(jnp.int32, (N_HEADS, QK_WIDTH), 1) // LANES
    own_head = head_of_lane == lax.broadcasted_iota(jnp.int32, (N_HEADS, QK_WIDTH), 0)
    pieces = []
    for q_ref in (q1_ref, q2_ref):
        for t in range(t_new):
            qt = jnp.broadcast_to(q_ref[t:t + 1, :], (N_HEADS, QK_WIDTH))
            pieces.append(jnp.where(own_head, qt, 0.0))
    qbd = jnp.concatenate(pieces, axis=0).astype(BF16)

    def update(s, v):
        m_prev = m_sc[...]
        m_new = jnp.maximum(m_prev, jnp.max(s, axis=-1, keepdims=True))
        alpha = jnp.exp2(m_prev - m_new)
        p = jnp.exp2(s - m_new)
        l_sc[...] = alpha * l_sc[...] + jnp.sum(p, axis=-1, keepdims=True)
        acc_sc[...] = alpha * acc_sc[...] + _dot(p.astype(BF16), v)
        m_sc[...] = m_new

    for i in range(np_):
        kc_sc[:, PAGE_SIZE * i:PAGE_SIZE * (i + 1)] = k_pages[i][...].astype(BF16)
        for h in range(N_HEADS):
            vh = v_pages[i][pl.ds(h, PAGE_SIZE, stride=N_HEADS), :]
            vc_sc[PAGE_SIZE * i:PAGE_SIZE * (i + 1), V_DIM * h:V_DIM * (h + 1)] = vh.astype(BF16)
    update(_dot(qbd, kc_sc[...]), vc_sc[...])

    @pl.when(j == pl.num_programs(1) - 1)
    def _():
        s = _dot_nt(qbd, kn_ref[...].astype(BF16))
        r = lax.broadcasted_iota(jnp.int32, s.shape, 0)
        tok = (r // N_HEADS) % t_new
        col = lax.broadcasted_iota(jnp.int32, s.shape, 1)
        s = jnp.where(col <= tok, s, -jnp.inf)
        update(s, vn_ref[...].astype(BF16))
        o = acc_sc[...] / l_sc[...]
        lam = _lambda(lp_ref[...], lam_init)
        half = rows // 2
        d = o[:half] - lam * o[half:]
        for t in range(t_new):
            blk = jnp.where(own_head, d[N_HEADS * t:N_HEADS * (t + 1)], 0.0)
            o_ref[t:t + 1, :] = jnp.sum(blk, axis=0, keepdims=True)


def _paged_attention(lp, page_table, q1, q2, k_new, v_new, cache_k2, cache_v2, lam_init):
    bd, t_new, _ = q1.shape
    n_pages = page_table.shape[1]
    np_ = PAGES_PER_STEP
    rows = 2 * t_new * N_HEADS
    chunk = np_ * PAGE_SIZE
    seq = lambda b, j, pt: (b, 0, 0)
    qspec = pl.BlockSpec((None, t_new, QK_WIDTH), seq)
    nspec = pl.BlockSpec((None, PAGE_SIZE, QK_WIDTH), seq)

    def k_page_spec(i):
        return pl.BlockSpec((QK_WIDTH, PAGE_SIZE), lambda b, j, pt: (pt[b, j * np_ + i], 0))

    def v_page_spec(i):
        return pl.BlockSpec((PAGE_SIZE * N_HEADS, V_DIM), lambda b, j, pt: (pt[b, j * np_ + i], 0))

    grid_spec = pltpu.PrefetchScalarGridSpec(
        num_scalar_prefetch=1,
        grid=(bd, n_pages // np_),
        in_specs=([pl.BlockSpec((4, HEAD_DIM), lambda b, j, pt: (0, 0)), qspec, qspec, nspec, nspec]
                  + [k_page_spec(i) for i in range(np_)] + [v_page_spec(i) for i in range(np_)]),
        out_specs=qspec,
        scratch_shapes=[pltpu.VMEM((rows, 1), F32), pltpu.VMEM((rows, 1), F32),
                        pltpu.VMEM((rows, ATTN_WIDTH), F32),
                        pltpu.VMEM((QK_WIDTH, chunk), BF16), pltpu.VMEM((chunk, ATTN_WIDTH), BF16)],
    )
    return pl.pallas_call(
        functools.partial(_paged_kernel, t_new=t_new, lam_init=lam_init),
        grid_spec=grid_spec,
        out_shape=jax.ShapeDtypeStruct((bd, t_new, ATTN_WIDTH), F32),
        compiler_params=_params(("parallel", "arbitrary"), 48),
        name="paged_diff_attn",
    )(page_table, lp, q1, q2, k_new, v_new, *([cache_k2] * np_), *([cache_v2] * np_))


def _post_kernel(*refs, tm, tiles_per_seq, t_short, alpha, lam_init):
    long_seq = tiles_per_seq > 0
    (x_ref, o_ref, w2_ref, wa_ref, wc_ref, wo_ref, wr_ref, g_ref, cw_ref, cb_ref,
     l1g_ref, l1b_ref, br_ref) = refs[:13]
    if long_seq:
        st_ref, x1_ref, ti_ref, tg_ref, nc_ref, carry = refs[13:]
    else:
        h1_ref, h2_ref, x1_ref, ti_ref, tg_ref, u_ref = refs[13:]

    x = x_ref[...]
    h = _dot(x.astype(BF16), w2_ref[...])
    c0 = CONV_CH
    bg, cg, xc = h[:, 0:c0], h[:, c0:2 * c0], h[:, 2 * c0:3 * c0]
    ga = h[:, 3 * c0:3 * c0 + D_MODEL]
    gc = h[:, 3 * c0 + D_MODEL:]

    u = cg * xc
    row = lax.broadcasted_iota(jnp.int32, u.shape, 0)
    if long_seq:
        first = pl.program_id(0) % tiles_per_seq == 0
        st = st_ref[...]
        prev2 = jnp.where(first, st[0:1], carry[0:1])
        prev1 = jnp.where(first, st[1:2], carry[1:2])
        u1 = jnp.where(row == 0, prev1, pltpu.roll(u, 1, 0))
        u2 = jnp.where(row == 0, prev2, jnp.where(row == 1, prev1, pltpu.roll(u, 2, 0)))
        carry[0:2, :] = u[tm - 2:tm]
        nc_ref[...] = u[tm - 2:tm]
    else:
        tpos = row % t_short
        u1 = jnp.where(tpos >= 1, pltpu.roll(u, 1, 0), h1_ref[...])
        u2 = jnp.where(tpos >= 2, pltpu.roll(u, 2, 0), h2_ref[...])
        u_ref[...] = u
    cw = cw_ref[...]
    y_c = bg * (cb_ref[...] + u2 * cw[0:1] + u1 * cw[1:2] + u * cw[2:3])
    y_c = _dot(y_c.astype(BF16), wc_ref[...])

    o = o_ref[...]
    gain = g_ref[...]
    heads = []
    for hd in range(N_HEADS):
        oh = o[:, V_DIM * hd:V_DIM * (hd + 1)]
        ms = jnp.mean(oh * oh, axis=-1, keepdims=True)
        heads.append(oh * lax.rsqrt(ms + LN_EPS) * gain * (1.0 - lam_init))
    y_a = _dot(jnp.concatenate(heads, axis=1).astype(BF16), wa_ref[...])

    mix = jax.nn.sigmoid(ga) * y_a + jax.nn.sigmoid(gc) * y_c
    z = alpha * x + _dot(mix.astype(BF16), wo_ref[...])
    mu = jnp.mean(z, axis=-1, keepdims=True)
    zc = z - mu
    var = jnp.mean(zc * zc, axis=-1, keepdims=True)
    x1 = zc * lax.rsqrt(var + LN_EPS) * l1g_ref[...] + l1b_ref[...]
    x1_ref[...] = x1

    logits = _dot(x1.astype(BF16), wr_ref[...]) + br_ref[...]
    lane = lax.broadcasted_iota(jnp.int32, logits.shape, 1)
    logits = jnp.where(lane < N_EXPERTS, logits, -jnp.inf)
    idx_out = jnp.zeros(logits.shape, jnp.int32)
    val_out = jnp.zeros(logits.shape, F32)
    top0 = None
    denom = jnp.zeros((tm, 1), F32)
    for k in range(TOP_K):
        mx = jnp.max(logits, axis=-1, keepdims=True)
        idx = jnp.min(jnp.where(logits == mx, lane, LANES), axis=-1, keepdims=True)
        if k == 0:
            top0 = mx
        e = jnp.exp(mx - top0)
        denom = denom + e
        idx_out = jnp.where(lane == k, idx, idx_out)
        val_out = jnp.where(lane == k, e, val_out)
        logits = jnp.where(lane == idx, -jnp.inf, logits)
    ti_ref[...] = idx_out
    tg_ref[...] = val_out / denom


def _post_block(x2d, o2d, weights, vecs, *, tm, tiles_per_seq=0, conv_state=None, hist=None, t_short=0,
                alpha, lam_init):
    n = x2d.shape[0]
    row = lambda i: (i, 0)
    full = lambda a: pl.BlockSpec(a.shape, lambda i: (0,) * a.ndim)
    in_specs = [pl.BlockSpec((tm, D_MODEL), row), pl.BlockSpec((tm, ATTN_WIDTH), row)]
    in_specs += [full(w) for w in weights] + [full(v) for v in vecs]
    out_specs = [pl.BlockSpec((tm, D_MODEL), row), pl.BlockSpec((tm, LANES), row), pl.BlockSpec((tm, LANES), row)]
    out_shape = [jax.ShapeDtypeStruct((n, D_MODEL), F32), jax.ShapeDtypeStruct((n, LANES), jnp.int32),
                 jax.ShapeDtypeStruct((n, LANES), F32)]
    scratch = []
    if tiles_per_seq > 0:
        extra = [conv_state]
        in_specs.append(pl.BlockSpec((None, CONV_K - 1, CONV_CH), lambda i: (i // tiles_per_seq, 0, 0)))
        out_specs.append(pl.BlockSpec((None, CONV_K - 1, CONV_CH), lambda i: (i // tiles_per_seq, 0, 0)))
        out_shape.append(jax.ShapeDtypeStruct(conv_state.shape, F32))
        scratch.append(pltpu.VMEM((8, CONV_CH), F32))
    else:
        extra = list(hist)
        in_specs += [pl.BlockSpec((tm, CONV_CH), row)] * 2
        out_specs.append(pl.BlockSpec((tm, CONV_CH), row))
        out_shape.append(jax.ShapeDtypeStruct((n, CONV_CH), F32))
    return pl.pallas_call(
        functools.partial(_post_kernel, tm=tm, tiles_per_seq=tiles_per_seq, t_short=t_short, alpha=alpha,
                          lam_init=lam_init),
        grid=(n // tm,),
        in_specs=in_specs,
        out_specs=out_specs,
        out_shape=out_shape,
        scratch_shapes=scratch,
        compiler_params=_params(("arbitrary",), 56),
        name="post_block",
    )(x2d, o2d, *weights, *vecs, *extra)


def _route(top_i, tm):
    n = top_i.shape[0]
    a = n * TOP_K
    flat_e = top_i.reshape(a)
    onehot = (flat_e[:, None] == jnp.arange(N_EXPERTS, dtype=jnp.int32)[None, :]).astype(jnp.int32)
    csum = jnp.cumsum(onehot, axis=0)
    rank = jnp.take_along_axis(csum, flat_e[:, None], axis=1)[:, 0] - 1
    counts = csum[-1]
    padded = (counts + tm - 1) // tm * tm
    pad_end = jnp.cumsum(padded)
    pad_start = pad_end - padded
    slot = (pad_start[flat_e] + rank).astype(jnp.int32)
    n_tiles = (a + N_EXPERTS * (tm - 1) + tm - 1) // tm
    tile_start = jnp.arange(n_tiles, dtype=jnp.int32) * tm
    tile_e = jnp.minimum(jnp.sum((pad_end[None, :] <= tile_start[:, None]).astype(jnp.int32), axis=1),
                         N_EXPERTS - 1).astype(jnp.int32)
    n_used = (pad_end[-1] // tm).astype(jnp.int32).reshape(1)
    return slot, tile_e, n_used, n_tiles * tm


def _dispatch_kernel(slot_ref, x_ref, xs_in_hbm, xs_hbm, sem, *, rows):
    del xs_in_hbm

    def issue(r, c):
        for k in range(TOP_K):
            pltpu.make_async_copy(x_ref.at[pl.ds(r, 1), :],
                                  xs_hbm.at[pl.ds(slot_ref[r * TOP_K + k], 1), :], sem).start()
        return c

    lax.fori_loop(0, rows, issue, 0)
    for _ in range(TOP_K):
        pltpu.make_async_copy(x_ref, xs_hbm.at[pl.ds(0, rows), :], sem).wait()


def _dispatch_rows(x1, slot, xs, rows):
    n = x1.shape[0]
    return pl.pallas_call(
        functools.partial(_dispatch_kernel, rows=rows),
        grid=(n // rows,),
        in_specs=[pl.BlockSpec((rows * TOP_K,), lambda i: (i,), memory_space=pltpu.SMEM),
                  pl.BlockSpec((rows, D_MODEL), lambda i: (i, 0)), pl.BlockSpec(memory_space=pl.ANY)],
        out_specs=pl.BlockSpec(memory_space=pl.ANY),
        out_shape=jax.ShapeDtypeStruct(xs.shape, xs.dtype),
        scratch_shapes=[pltpu.SemaphoreType.DMA(())],
        input_output_aliases={2: 0},
        compiler_params=_params(("arbitrary",), 16),
        name="moe_dispatch",
    )(slot, x1, xs)


def _split_kernel(w_ref, p_ref, g_ref, u_ref):
    w = w_ref[...].astype(BF16)
    half = 2 * LANES
    for t in range(D_FF // half):
        y = _dot(w[:, 2 * half * t:2 * half * (t + 1)], p_ref[...])
        g_ref[:, half * t:half * (t + 1)] = y[:, :half].astype(BF16)
        u_ref[:, half * t:half * (t + 1)] = y[:, half:].astype(BF16)


def _split_gate_up(w_gate_up, rows):
    e, dm, f2 = w_gate_up.shape
    half = 2 * LANES
    r = lax.broadcasted_iota(jnp.int32, (2 * half, 2 * half), 0)
    c = lax.broadcasted_iota(jnp.int32, (2 * half, 2 * half), 1)
    perm = jnp.where(c < half, r == 2 * c, r == 2 * (c - half) + 1).astype(BF16)
    out = pl.BlockSpec((None, rows, f2 // 2), lambda ei, i: (ei, i, 0))
    return pl.pallas_call(
        _split_kernel,
        grid=(e, dm // rows),
        in_specs=[pl.BlockSpec((None, rows, f2), lambda ei, i: (ei, i, 0)),
                  pl.BlockSpec((2 * half, 2 * half), lambda ei, i: (0, 0))],
        out_specs=[out, out],
        out_shape=[jax.ShapeDtypeStruct((e, dm, f2 // 2), BF16)] * 2,
        compiler_params=_params(("parallel", "parallel"), 32),
        name="split_gate_up",
    )(w_gate_up, perm)


def _ffn_kernel(te_ref, nu_ref, xs_ref, wg_ref, wu_ref, bg_ref, bu_ref, wd_ref, bd_ref, o_ref):
    i = pl.program_id(0)

    @pl.when(i < nu_ref[0])
    def _():
        x = xs_ref[...].astype(BF16)
        g = jnp.minimum(_dot(x, wg_ref[...]) + bg_ref[...], SWIGLU_LIMIT)
        u = jnp.clip(_dot(x, wu_ref[...]) + bu_ref[...], -SWIGLU_LIMIT, SWIGLU_LIMIT)
        act = (u + 1.0) * g * jax.nn.sigmoid(SWIGLU_ALPHA * g)
        o_ref[...] = _dot(act.astype(BF16), wd_ref[...]) + bd_ref[...]

    @pl.when(i >= nu_ref[0])
    def _():
        o_ref[...] = jnp.zeros(o_ref.shape, F32)


def _expert_ffn(xs, tile_e, n_used, wg, wu, bg, bu, wd, bd, tm):
    n_slots = xs.shape[0]
    row = lambda i, te, nu: (i, 0)
    wsel = lambda i, te, nu: (te[i], 0, 0)
    grid_spec = pltpu.PrefetchScalarGridSpec(
        num_scalar_prefetch=2,
        grid=(n_slots // tm,),
        in_specs=[pl.BlockSpec((tm, D_MODEL), row),
                  pl.BlockSpec((None, D_MODEL, D_FF), wsel), pl.BlockSpec((None, D_MODEL, D_FF), wsel),
                  pl.BlockSpec((None, 1, D_FF), wsel), pl.BlockSpec((None, 1, D_FF), wsel),
                  pl.BlockSpec((None, D_FF, D_MODEL), wsel), pl.BlockSpec((None, 1, D_MODEL), wsel)],
        out_specs=pl.BlockSpec((tm, D_MODEL), row),
    )
    return pl.pallas_call(
        _ffn_kernel,
        grid_spec=grid_spec,
        out_shape=jax.ShapeDtypeStruct((n_slots, D_MODEL), F32),
        compiler_params=_params(("arbitrary",), 48),
        name="moe_ffn",
    )(tile_e, n_used, xs, wg, wu, bg, bu, wd, bd)


def _combine_kernel(slot_ref, next_slot_ref, x1_ref, gate_ref, y_hbm, g_ref, b_ref, o_ref, buf, sem, *, rows, alpha):
    i = pl.program_id(0)
    cur = lax.rem(i, 2)

    def issue(s_ref, b):
        def body(r, c):
            for k in range(TOP_K):
                pltpu.make_async_copy(y_hbm.at[pl.ds(s_ref[r * TOP_K + k], 1), :],
                                      buf.at[b, k, pl.ds(r, 1), :], sem.at[b]).start()
            return c

        lax.fori_loop(0, rows, body, 0)

    @pl.when(i == 0)
    def _():
        issue(slot_ref, 0)

    @pl.when(i + 1 < pl.num_programs(0))
    def _():
        issue(next_slot_ref, 1 - cur)

    for k in range(TOP_K):
        pltpu.make_async_copy(y_hbm.at[pl.ds(0, rows), :], buf.at[cur, k], sem.at[cur]).wait()
    gate = gate_ref[...]
    f = ((gate[:, 0:1] * buf[cur, 0] + gate[:, 1:2] * buf[cur, 1])
         + (gate[:, 2:3] * buf[cur, 2] + gate[:, 3:4] * buf[cur, 3]))
    z = alpha * x1_ref[...] + f
    mu = jnp.mean(z, axis=-1, keepdims=True)
    zc = z - mu
    var = jnp.mean(zc * zc, axis=-1, keepdims=True)
    o_ref[...] = zc * lax.rsqrt(var + LN_EPS) * g_ref[...] + b_ref[...]


def _combine(x1, gate, slot, yb, ln_g, ln_b, rows, alpha):
    n = x1.shape[0]
    steps = n // rows
    row = lambda i: (i, 0)
    vec = pl.BlockSpec((1, D_MODEL), lambda i: (0, 0))
    return pl.pallas_call(
        functools.partial(_combine_kernel, rows=rows, alpha=alpha),
        grid=(steps,),
        in_specs=[pl.BlockSpec((rows * TOP_K,), lambda i: (i,), memory_space=pltpu.SMEM),
                  pl.BlockSpec((rows * TOP_K,), lambda i: (jnp.minimum(i + 1, steps - 1),), memory_space=pltpu.SMEM),
                  pl.BlockSpec((rows, D_MODEL), row), pl.BlockSpec((rows, LANES), row),
                  pl.BlockSpec(memory_space=pl.ANY), vec, vec],
        out_specs=pl.BlockSpec((rows, D_MODEL), row),
        out_shape=jax.ShapeDtypeStruct((n, D_MODEL), F32),
        scratch_shapes=[pltpu.VMEM((2, TOP_K, rows, D_MODEL), F32), pltpu.SemaphoreType.DMA((2,))],
        compiler_params=_params(("arbitrary",), 32),
        name="moe_combine",
    )(slot, slot, x1, gate, yb, ln_g, ln_b)


def kernel(x_prompt, x_sample, cache_k, cache_v, state_conv, page_table, w_in, lambda_q1, lambda_k1,
           lambda_q2, lambda_k2, subln_g, w_attn_out, w_conv, b_conv, w_conv_out, w_o, ln1_g, ln1_b,
           w_router, b_router, w_gate_up, b_gate_up, w_down, b_down, ln2_g, ln2_b):
    bp, s, d = x_prompt.shape
    bd, t_new, _ = x_sample.shape
    depth = w_in.shape[0]
    n_phys = cache_k.shape[1]
    past = page_table.shape[1] * PAGE_SIZE
    alpha = (2.0 * depth) ** 0.25
    np_rows, ns_rows = bp * s, bd * t_new
    tm_p = 512
    tq = 512

    tab_p = _rope_tables(jnp.arange(s))
    tab_s = _rope_tables(jnp.tile(past + jnp.arange(t_new), bd))

    cache_k2 = jnp.transpose(cache_k, (0, 1, 3, 4, 5, 2)).reshape(depth * n_phys * QK_WIDTH, PAGE_SIZE)
    cache_v2 = cache_v.reshape(depth * n_phys * PAGE_SIZE * N_HEADS, V_DIM)
    hp = x_prompt.reshape(np_rows, d)
    hs = x_sample.reshape(ns_rows, d)
    outs = {k: [] for k in ("kp", "vp", "cp", "ks", "vs", "cs")}
    for l in range(depth):
        lam_init = 0.8 - 0.6 * math.exp(-0.3 * l)
        lp = jnp.concatenate([lambda_q1[l:l + 1], lambda_k1[l:l + 1], lambda_q2[l:l + 1], lambda_k2[l:l + 1]], axis=0)
        w_l = w_in[l]
        w_qkv = w_l[:, :QKV_COLS].astype(BF16)
        w_rest = w_l[:, QKV_COLS:].astype(BF16)
        w_r = jnp.pad(w_router[l], ((0, 0), (0, LANES - N_EXPERTS))).astype(BF16)
        b_r = jnp.pad(b_router[l], (0, LANES - N_EXPERTS)).reshape(1, LANES)
        weights = (w_rest, w_attn_out[l].astype(BF16), w_conv_out[l].astype(BF16), w_o[l].astype(BF16), w_r)
        vecs = (subln_g[l].reshape(1, V_DIM), w_conv[l],
                b_conv[l].reshape(1, CONV_CH), ln1_g[l].reshape(1, d), ln1_b[l].reshape(1, d), b_r)

        q1, q2, kf, kb, vf, vt = _qkv_proj(hp, w_qkv, tab_p, tm_p)
        shp = (bp, s, QK_WIDTH)
        o_p = _flash_attention(lp, q1.reshape(shp), q2.reshape(shp), kb.reshape(shp), vt, lam_init, tq)
        x1_p, ti_p, tg_p, nc_p = _post_block(
            hp, o_p.reshape(np_rows, ATTN_WIDTH), weights, vecs, tm=tm_p // 2, tiles_per_seq=s // (tm_p // 2),
            conv_state=jnp.zeros((bp, CONV_K - 1, CONV_CH), F32), alpha=alpha, lam_init=lam_init)
        outs["kp"].append(kf.reshape(bp, s, N_HEADS, 2, HEAD_DIM))
        outs["vp"].append(vf.reshape(bp, s, N_HEADS, V_DIM))
        outs["cp"].append(nc_p)

        q1s, q2s, kfs, _, vfs, _ = _qkv_proj(hs, w_qkv, tab_s, ns_rows)
        seq3 = lambda a: a.reshape(bd, t_new, QK_WIDTH)
        pad_page = lambda a: jnp.pad(seq3(a), ((0, 0), (0, PAGE_SIZE - t_new), (0, 0)))
        o_s = _paged_attention(
            lp, page_table + l * n_phys, seq3(q1s).astype(F32), seq3(q2s).astype(F32), pad_page(kfs), pad_page(vfs),
            cache_k2, cache_v2, lam_init)
        st = state_conv[l]
        zero = jnp.zeros((bd, 1, CONV_CH), F32)
        hist1 = jnp.concatenate([st[:, 1:2]] + [zero] * (t_new - 1), axis=1).reshape(ns_rows, CONV_CH)
        hist2 = jnp.concatenate([st[:, 0:1], st[:, 1:2]] + [zero] * (t_new - 2), axis=1).reshape(ns_rows, CONV_CH)
        x1_s, ti_s, tg_s, u_s = _post_block(
            hs, o_s.reshape(ns_rows, ATTN_WIDTH), weights, vecs, tm=ns_rows, hist=(hist1, hist2), t_short=t_new,
            alpha=alpha, lam_init=lam_init)
        outs["ks"].append(kfs.reshape(bd, t_new, N_HEADS, 2, HEAD_DIM))
        outs["vs"].append(vfs.reshape(bd, t_new, N_HEADS, V_DIM))
        outs["cs"].append(u_s.reshape(bd, t_new, CONV_CH)[:, t_new - (CONV_K - 1):])

        top_i = jnp.concatenate([ti_p[:, :TOP_K], ti_s[:, :TOP_K]], axis=0)
        slot, tile_e, n_used, n_slots = _route(top_i, FFN_TILE)
        slot_p, slot_s = slot[:np_rows * TOP_K], slot[np_rows * TOP_K:]
        xs = _dispatch_rows(x1_p, slot_p, jnp.zeros((n_slots, d), F32), 1024)
        xs = _dispatch_rows(x1_s, slot_s, xs, ns_rows)
        w_gate, w_up = _split_gate_up(w_gate_up[l], 512)
        bgu = b_gate_up[l]
        yb = _expert_ffn(
            xs, tile_e, n_used, w_gate, w_up,
            bgu[:, None, 0::2], bgu[:, None, 1::2], w_down[l].astype(BF16), b_down[l][:, None, :], FFN_TILE)
        g2, b2 = ln2_g[l].reshape(1, d), ln2_b[l].reshape(1, d)
        hp = _combine(x1_p, tg_p, slot_p, yb, g2, b2, 256, alpha)
        hs = _combine(x1_s, tg_s, slot_s, yb, g2, b2, ns_rows, alpha)

    return (hp.reshape(bp, s, d), hs.reshape(bd, t_new, d),
            jnp.stack(outs["kp"]), jnp.stack(outs["vp"]), jnp.stack(outs["cp"]),
            jnp.stack(outs["ks"]), jnp.stack(outs["vs"]), jnp.stack(outs["cs"]))
```

```python
import functools
import math

import jax
import jax.numpy as jnp
from jax import lax
from jax.experimental import pallas as pl
from jax.experimental.pallas import tpu as pltpu

F32 = jnp.float32
BF16 = jnp.bfloat16

D_MODEL = 1024
N_HEADS = 8
HEAD_DIM = 64
V_DIM = 2 * HEAD_DIM
QK_WIDTH = N_HEADS * 2 * HEAD_DIM
ATTN_WIDTH = N_HEADS * V_DIM
ROT_DIM = HEAD_DIM // 4
ROPE_THETA = 500000.0
CONV_CH = D_MODEL // 2
CONV_K = 3
N_EXPERTS = 32
TOP_K = 4
D_FF = D_MODEL
SWIGLU_LIMIT = 7.0
SWIGLU_ALPHA = 1.702
LN_EPS = 1e-5
PAGE_SIZE = 128
LANES = 128
MIB = 1024 * 1024

QKV_COLS = 2 * QK_WIDTH + ATTN_WIDTH
REST_COLS = 3 * CONV_CH + 2 * D_MODEL

PAGES_PER_STEP = 8
FFN_TILE = 512
LOG2_E = math.log2(math.e)


def _params(semantics, vmem_mib):
    return pltpu.CompilerParams(dimension_semantics=semantics, vmem_limit_bytes=vmem_mib * MIB)


def _dot(a, b):
    return jnp.dot(a, b, preferred_element_type=F32)


def _dot_nt(a, b):
    return lax.dot_general(a, b, (((1,), (1,)), ((), ())), preferred_element_type=F32)


def _rope_tables(pos):
    inv = ROPE_THETA ** (-jnp.arange(0, ROT_DIM, 2, dtype=F32) / ROT_DIM)
    ang = pos.astype(F32)[:, None] * inv[None, :]
    cos, sin = jnp.cos(ang), jnp.sin(ang)
    n = pos.shape[0]
    half = ROT_DIM // 2
    rest = HEAD_DIM - ROT_DIM
    c = jnp.concatenate([cos, cos, jnp.ones((n, rest), F32)], axis=1)
    s_up = jnp.concatenate([-sin, jnp.zeros((n, half + rest), F32)], axis=1)
    s_dn = jnp.concatenate([jnp.zeros((n, half), F32), sin, jnp.zeros((n, rest), F32)], axis=1)
    rep = LANES // HEAD_DIM
    return tuple(jnp.tile(t, (1, rep)) for t in (c, s_up, s_dn))


def _qkv_kernel(x_ref, w_ref, c_ref, su_ref, sd_ref, q1_ref, q2_ref, kf_ref, kb_ref, vf_ref, vt_ref):
    x = x_ref[...].astype(BF16)
    c, s_up, s_dn = c_ref[...], su_ref[...], sd_ref[...]
    half = ROT_DIM // 2
    first_map = lax.broadcasted_iota(jnp.int32, c.shape, 1) < HEAD_DIM
    scale = HEAD_DIM ** -0.5 * LOG2_E

    def rope(blk):
        return blk * c + pltpu.roll(blk, LANES - half, 1) * s_up + pltpu.roll(blk, half, 1) * s_dn

    hq = _dot(x, w_ref[:, 0:QK_WIDTH])
    for j in range(QK_WIDTH // LANES):
        sl = slice(LANES * j, LANES * (j + 1))
        r = rope(hq[:, sl]) * scale
        q1_ref[:, sl] = jnp.where(first_map, r, 0.0).astype(BF16)
        q2_ref[:, sl] = jnp.where(first_map, 0.0, r).astype(BF16)
    hk = _dot(x, w_ref[:, QK_WIDTH:2 * QK_WIDTH])
    for j in range(QK_WIDTH // LANES):
        sl = slice(LANES * j, LANES * (j + 1))
        r = rope(hk[:, sl])
        kf_ref[:, sl] = r
        kb_ref[:, sl] = r.astype(BF16)
    hv = _dot(x, w_ref[:, 2 * QK_WIDTH:QKV_COLS])
    vf_ref[...] = hv
    vt_ref[...] = hv.T.astype(BF16)


def _qkv_proj(x2d, w_qkv, tables, tm):
    n = x2d.shape[0]
    seq = tables[0].shape[0]
    n_pos_blocks = seq // tm
    row = lambda i: (i, 0)
    tab = pl.BlockSpec((tm, LANES), lambda i: (i % n_pos_blocks, 0))
    wide = lambda: pl.BlockSpec((tm, QK_WIDTH), row)
    shp = lambda dt: jax.ShapeDtypeStruct((n, QK_WIDTH), dt)
    vt_spec = pl.BlockSpec((None, ATTN_WIDTH, tm), lambda i: (i // n_pos_blocks, 0, i % n_pos_blocks))
    return pl.pallas_call(
        _qkv_kernel,
        grid=(n // tm,),
        in_specs=[pl.BlockSpec((tm, D_MODEL), row),
                  pl.BlockSpec((D_MODEL, QKV_COLS), lambda i: (0, 0)),
                  tab, tab, tab],
        out_specs=[wide() for _ in range(5)] + [vt_spec],
        out_shape=[shp(BF16), shp(BF16), shp(F32), shp(BF16), shp(F32),
                   jax.ShapeDtypeStruct((n // seq, ATTN_WIDTH, seq), BF16)],
        compiler_params=_params(("parallel",), 48),
        name="qkv_proj",
    )(x2d, w_qkv, *tables)


def _lambda(lp, lam_init):
    a = jnp.sum(lp[0:1] * lp[1:2], axis=-1, keepdims=True)
    b = jnp.sum(lp[2:3] * lp[3:4], axis=-1, keepdims=True)
    return jnp.exp(a) - jnp.exp(b) + lam_init


FLASH_GROUP = 256


def _flash_kernel(lp_ref, q1_ref, q2_ref, k_ref, vt_ref, o_ref, m_sc, l_sc, acc_sc, sa_sc, sb_sc, *, tq, lam_init):
    qi = pl.program_id(2)
    m_sc[...] = jnp.full(m_sc.shape, -jnp.inf, F32)
    l_sc[...] = jnp.zeros(l_sc.shape, F32)
    acc_sc[...] = jnp.zeros(acc_sc.shape, F32)
    gw = FLASH_GROUP

    def scores(ki, s_sc):
        k = k_ref[pl.ds(pl.multiple_of(ki * tq, tq), tq), :]
        s_sc[:, 0:tq] = _dot_nt(k, q1_ref[...])
        s_sc[:, tq:2 * tq] = _dot_nt(k, q2_ref[...])

    def consume(ki, s_sc, masked):
        vt = vt_ref[:, pl.ds(pl.multiple_of(ki * tq, tq), tq)]
        for m in range(2):
            probs, alphas = [], []
            for c0 in range(0, tq, gw):
                cols = slice(m * tq + c0, m * tq + c0 + gw)
                s = s_sc[:, cols]
                if masked:
                    key = lax.broadcasted_iota(jnp.int32, s.shape, 0)
                    qry = lax.broadcasted_iota(jnp.int32, s.shape, 1) + c0
                    s = jnp.where(key <= qry, s, -jnp.inf)
                m_prev = m_sc[:, cols]
                m_new = jnp.maximum(m_prev, jnp.max(s, axis=0, keepdims=True))
                alpha = jnp.exp2(m_prev - m_new)
                p = jnp.exp2(s - m_new)
                l_sc[:, cols] = alpha * l_sc[:, cols] + jnp.sum(p, axis=0, keepdims=True)
                m_sc[:, cols] = m_new
                probs.append(p.astype(BF16))
                alphas.append(alpha)
            cols = slice(m * tq, (m + 1) * tq)
            acc_sc[:, cols] = (jnp.concatenate(alphas, axis=1) * acc_sc[:, cols]
                               + _dot(vt, jnp.concatenate(probs, axis=1)))

    scores(0, sa_sc)

    def pair(j, carry):
        scores(2 * j + 1, sb_sc)
        consume(2 * j, sa_sc, False)
        scores(2 * j + 2, sa_sc)
        consume(2 * j + 1, sb_sc, False)
        return carry

    lax.fori_loop(0, qi // 2, pair, 0)

    @pl.when(qi % 2 == 0)
    def _():
        consume(qi, sa_sc, True)

    @pl.when(qi % 2 == 1)
    def _():
        scores(qi, sb_sc)
        consume(qi - 1, sa_sc, False)
        consume(qi, sb_sc, True)

    o = acc_sc[...] / l_sc[...]
    lam = _lambda(lp_ref[...], lam_init)
    o_ref[...] = (o[:, :tq] - lam * o[:, tq:]).T


def _flash_attention(lp, q1, q2, kb, vt, lam_init, tq):
    b, s, _ = q1.shape
    qspec = pl.BlockSpec((None, tq, LANES), lambda bi, h, qi: (bi, qi, h))
    kspec = pl.BlockSpec((None, s, LANES), lambda bi, h, qi: (bi, 0, h))
    vspec = pl.BlockSpec((None, V_DIM, s), lambda bi, h, qi: (bi, h, 0))
    return pl.pallas_call(
        functools.partial(_flash_kernel, tq=tq, lam_init=lam_init),
        grid=(b, N_HEADS, s // tq),
        in_specs=[pl.BlockSpec((4, HEAD_DIM), lambda bi, h, qi: (0, 0)), qspec, qspec, kspec, vspec],
        out_specs=qspec,
        out_shape=jax.ShapeDtypeStruct((b, s, ATTN_WIDTH), F32),
        scratch_shapes=[pltpu.VMEM((1, 2 * tq), F32), pltpu.VMEM((1, 2 * tq), F32),
                        pltpu.VMEM((V_DIM, 2 * tq), F32),
                        pltpu.VMEM((tq, 2 * tq), F32), pltpu.VMEM((tq, 2 * tq), F32)],
        compiler_params=_params(("parallel", "parallel", "arbitrary"), 48),
        name="flash_diff_attn",
    )(lp, q1, q2, kb, vt)


def _paged_kernel(pt_ref, lp_ref, q1_ref, q2_ref, kn_ref, vn_ref, *rest, t_new, lam_init):
    np_ = PAGES_PER_STEP
    k_pages, v_pages = rest[:np_], rest[np_:2 * np_]
    o_ref, m_sc, l_sc, acc_sc, kc_sc, vc_sc = rest[2 * np_:]
    j = pl.program_id(1)
    rows = 2 * t_new * N_HEADS

    @pl.when(j == 0)
    def _():
        m_sc[...] = jnp.full(m_sc.shape, -jnp.inf, F32)
        l_sc[...] = jnp.zeros(l_sc.shape, F32)
        acc_sc[...] = jnp.zeros(acc_sc.shape, F32)

    head_of_lane = lax.broadcasted_iota(jnp.int32, (N_HEADS, QK_WIDTH), 1) // LANES
    own_head = head_of_lane == lax.broadcasted_iota(jnp.int32, (N_HEADS, QK_WIDTH), 0)
    pieces = []
    for q_ref in (q1_ref, q2_ref):
        for t in range(t_new):
            qt = jnp.broadcast_to(q_ref[t:t + 1, :], (N_HEADS, QK_WIDTH))
            pieces.append(jnp.where(own_head, qt, 0.0))
    qbd = jnp.concatenate(pieces, axis=0).astype(BF16)

    def update(s, v):
        m_prev = m_sc[...]
        m_new = jnp.maximum(m_prev, jnp.max(s, axis=-1, keepdims=True))
        alpha = jnp.exp2(m_prev - m_new)
        p = jnp.exp2(s - m_new)
        l_sc[...] = alpha * l_sc[...] + jnp.sum(p, axis=-1, keepdims=True)
        acc_sc[...] = alpha * acc_sc[...] + _dot(p.astype(BF16), v)
        m_sc[...] = m_new

    for i in range(np_):
        kc_sc[:, PAGE_SIZE * i:PAGE_SIZE * (i + 1)] = k_pages[i][...].astype(BF16)
        for h in range(N_HEADS):
            vh = v_pages[i][pl.ds(h, PAGE_SIZE, stride=N_HEADS), :]
            vc_sc[PAGE_SIZE * i:PAGE_SIZE * (i + 1), V_DIM * h:V_DIM * (h + 1)] = vh.astype(BF16)
    update(_dot(qbd, kc_sc[...]), vc_sc[...])

    @pl.when(j == pl.num_programs(1) - 1)
    def _():
        s = _dot_nt(qbd, kn_ref[...].astype(BF16))
        r = lax.broadcasted_iota(jnp.int32, s.shape, 0)
        tok = (r // N_HEADS) % t_new
        col = lax.broadcasted_iota(jnp.int32, s.shape, 1)
        s = jnp.where(col <= tok, s, -jnp.inf)
        update(s, vn_ref[...].astype(BF16))
        o = acc_sc[...] / l_sc[...]
        lam = _lambda(lp_ref[...], lam_init)
        half = rows // 2
        d = o[:half] - lam * o[half:]
        for t in range(t_new):
            blk = jnp.where(own_head, d[N_HEADS * t:N_HEADS * (t + 1)], 0.0)
            o_ref[t:t + 1, :] = jnp.sum(blk, axis=0, keepdims=True)


def _paged_attention(lp, page_table, q1, q2, k_new, v_new, cache_k2, cache_v2, lam_init):
    bd, t_new, _ = q1.shape
    n_pages = page_table.shape[1]
    np_ = PAGES_PER_STEP
    rows = 2 * t_new * N_HEADS
    chunk = np_ * PAGE_SIZE
    seq = lambda b, j, pt: (b, 0, 0)
    qspec = pl.BlockSpec((None, t_new, QK_WIDTH), seq)
    nspec = pl.BlockSpec((None, PAGE_SIZE, QK_WIDTH), seq)

    def k_page_spec(i):
        return pl.BlockSpec((QK_WIDTH, PAGE_SIZE), lambda b, j, pt: (pt[b, j * np_ + i], 0))

    def v_page_spec(i):
        return pl.BlockSpec((PAGE_SIZE * N_HEADS, V_DIM), lambda b, j, pt: (pt[b, j * np_ + i], 0))

    grid_spec = pltpu.PrefetchScalarGridSpec(
        num_scalar_prefetch=1,
        grid=(bd, n_pages // np_),
        in_specs=([pl.BlockSpec((4, HEAD_DIM), lambda b, j, pt: (0, 0)), qspec, qspec, nspec, nspec]
                  + [k_page_spec(i) for i in range(np_)] + [v_page_spec(i) for i in range(np_)]),
        out_specs=qspec,
        scratch_shapes=[pltpu.VMEM((rows, 1), F32), pltpu.VMEM((rows, 1), F32),
                        pltpu.VMEM((rows, ATTN_WIDTH), F32),
                        pltpu.VMEM((QK_WIDTH, chunk), BF16), pltpu.VMEM((chunk, ATTN_WIDTH), BF16)],
    )
    return pl.pallas_call(
        functools.partial(_paged_kernel, t_new=t_new, lam_init=lam_init),
        grid_spec=grid_spec,
        out_shape=jax.ShapeDtypeStruct((bd, t_new, ATTN_WIDTH), F32),
        compiler_params=_params(("parallel", "arbitrary"), 48),
        name="paged_diff_attn",
    )(page_table, lp, q1, q2, k_new, v_new, *([cache_k2] * np_), *([cache_v2] * np_))


def _post_kernel(*refs, tm, tiles_per_seq, t_short, alpha, lam_init):
    long_seq = tiles_per_seq > 0
    (x_ref, o_ref, w2_ref, wa_ref, wc_ref, wo_ref, wr_ref, g_ref, cw_ref, cb_ref,
     l1g_ref, l1b_ref, br_ref) = refs[:13]
    if long_seq:
        st_ref, x1_ref, ti_ref, tg_ref, nc_ref, carry = refs[13:]
    else:
        h1_ref, h2_ref, x1_ref, ti_ref, tg_ref, u_ref = refs[13:]

    x = x_ref[...]
    h = _dot(x.astype(BF16), w2_ref[...])
    c0 = CONV_CH
    bg, cg, xc = h[:, 0:c0], h[:, c0:2 * c0], h[:, 2 * c0:3 * c0]
    ga = h[:, 3 * c0:3 * c0 + D_MODEL]
    gc = h[:, 3 * c0 + D_MODEL:]

    u = cg * xc
    row = lax.broadcasted_iota(jnp.int32, u.shape, 0)
    if long_seq:
        first = pl.program_id(0) % tiles_per_seq == 0
        st = st_ref[...]
        prev2 = jnp.where(first, st[0:1], carry[0:1])
        prev1 = jnp.where(first, st[1:2], carry[1:2])
        u1 = jnp.where(row == 0, prev1, pltpu.roll(u, 1, 0))
        u2 = jnp.where(row == 0, prev2, jnp.where(row == 1, prev1, pltpu.roll(u, 2, 0)))
        carry[0:2, :] = u[tm - 2:tm]
        nc_ref[...] = u[tm - 2:tm]
    else:
        tpos = row % t_short
        u1 = jnp.where(tpos >= 1, pltpu.roll(u, 1, 0), h1_ref[...])
        u2 = jnp.where(tpos >= 2, pltpu.roll(u, 2, 0), h2_ref[...])
        u_ref[...] = u
    cw = cw_ref[...]
    y_c = bg * (cb_ref[...] + u2 * cw[0:1] + u1 * cw[1:2] + u * cw[2:3])
    y_c = _dot(y_c.astype(BF16), wc_ref[...])

    o = o_ref[...]
    gain = g_ref[...]
    heads = []
    for hd in range(N_HEADS):
        oh = o[:, V_DIM * hd:V_DIM * (hd + 1)]
        ms = jnp.mean(oh * oh, axis=-1, keepdims=True)
        heads.append(oh * lax.rsqrt(ms + LN_EPS) * gain * (1.0 - lam_init))
    y_a = _dot(jnp.concatenate(heads, axis=1).astype(BF16), wa_ref[...])

    mix = jax.nn.sigmoid(ga) * y_a + jax.nn.sigmoid(gc) * y_c
    z = alpha * x + _dot(mix.astype(BF16), wo_ref[...])
    mu = jnp.mean(z, axis=-1, keepdims=True)
    zc = z - mu
    var = jnp.mean(zc * zc, axis=-1, keepdims=True)
    x1 = zc * lax.rsqrt(var + LN_EPS) * l1g_ref[...] + l1b_ref[...]
    x1_ref[...] = x1

    logits = _dot(x1.astype(BF16), wr_ref[...]) + br_ref[...]
    lane = lax.broadcasted_iota(jnp.int32, logits.shape, 1)
    logits = jnp.where(lane < N_EXPERTS, logits, -jnp.inf)
    idx_out = jnp.zeros(logits.shape, jnp.int32)
    val_out = jnp.zeros(logits.shape, F32)
    top0 = None
    denom = jnp.zeros((tm, 1), F32)
    for k in range(TOP_K):
        mx = jnp.max(logits, axis=-1, keepdims=True)
        idx = jnp.min(jnp.where(logits == mx, lane, LANES), axis=-1, keepdims=True)
        if k == 0:
            top0 = mx
        e = jnp.exp(mx - top0)
        denom = denom + e
        idx_out = jnp.where(lane == k, idx, idx_out)
        val_out = jnp.where(lane == k, e, val_out)
        logits = jnp.where(lane == idx, -jnp.inf, logits)
    ti_ref[...] = idx_out
    tg_ref[...] = val_out / denom


def _post_block(x2d, o2d, weights, vecs, *, tm, tiles_per_seq=0, conv_state=None, hist=None, t_short=0,
                alpha, lam_init):
    n = x2d.shape[0]
    row = lambda i: (i, 0)
    full = lambda a: pl.BlockSpec(a.shape, lambda i: (0,) * a.ndim)
    in_specs = [pl.BlockSpec((tm, D_MODEL), row), pl.BlockSpec((tm, ATTN_WIDTH), row)]
    in_specs += [full(w) for w in weights] + [full(v) for v in vecs]
    out_specs = [pl.BlockSpec((tm, D_MODEL), row), pl.BlockSpec((tm, LANES), row), pl.BlockSpec((tm, LANES), row)]
    out_shape = [jax.ShapeDtypeStruct((n, D_MODEL), F32), jax.ShapeDtypeStruct((n, LANES), jnp.int32),
                 jax.ShapeDtypeStruct((n, LANES), F32)]
    scratch = []
    if tiles_per_seq > 0:
        extra = [conv_state]
        in_specs.append(pl.BlockSpec((None, CONV_K - 1, CONV_CH), lambda i: (i // tiles_per_seq, 0, 0)))
        out_specs.append(pl.BlockSpec((None, CONV_K - 1, CONV_CH), lambda i: (i // tiles_per_seq, 0, 0)))
        out_shape.append(jax.ShapeDtypeStruct(conv_state.shape, F32))
        scratch.append(pltpu.VMEM((8, CONV_CH), F32))
    else:
        extra = list(hist)
        in_specs += [pl.BlockSpec((tm, CONV_CH), row)] * 2
        out_specs.append(pl.BlockSpec((tm, CONV_CH), row))
        out_shape.append(jax.ShapeDtypeStruct((n, CONV_CH), F32))
    return pl.pallas_call(
        functools.partial(_post_kernel, tm=tm, tiles_per_seq=tiles_per_seq, t_short=t_short, alpha=alpha,
                          lam_init=lam_init),
        grid=(n // tm,),
        in_specs=in_specs,
        out_specs=out_specs,
        out_shape=out_shape,
        scratch_shapes=scratch,
        compiler_params=_params(("arbitrary",), 56),
        name="post_block",
    )(x2d, o2d, *weights, *vecs, *extra)


def _route(top_i, tm):
    n = top_i.shape[0]
    a = n * TOP_K
    flat_e = top_i.reshape(a)
    onehot = (flat_e[:, None] == jnp.arange(N_EXPERTS, dtype=jnp.int32)[None, :]).astype(jnp.int32)
    csum = jnp.cumsum(onehot, axis=0)
    rank = jnp.take_along_axis(csum, flat_e[:, None], axis=1)[:, 0] - 1
    counts = csum[-1]
    padded = (counts + tm - 1) // tm * tm
    pad_end = jnp.cumsum(padded)
    pad_start = pad_end - padded
    slot = (pad_start[flat_e] + rank).astype(jnp.int32)
    n_tiles = (a + N_EXPERTS * (tm - 1) + tm - 1) // tm
    tile_start = jnp.arange(n_tiles, dtype=jnp.int32) * tm
    tile_e = jnp.minimum(jnp.sum((pad_end[None, :] <= tile_start[:, None]).astype(jnp.int32), axis=1),
                         N_EXPERTS - 1).astype(jnp.int32)
    n_used = (pad_end[-1] // tm).astype(jnp.int32).reshape(1)
    return slot, tile_e, n_used, n_tiles * tm


def _dispatch_kernel(slot_ref, x_ref, xs_in_hbm, xs_hbm, sem, *, rows):
    del xs_in_hbm

    def issue(r, c):
        for k in range(TOP_K):
            pltpu.make_async_copy(x_ref.at[pl.ds(r, 1), :],
                                  xs_hbm.at[pl.ds(slot_ref[r * TOP_K + k], 1), :], sem).start(priority=k % 2)
        return c

    lax.fori_loop(0, rows, issue, 0)
    for _ in range(TOP_K):
        pltpu.make_async_copy(x_ref, xs_hbm.at[pl.ds(0, rows), :], sem).wait()


def _dispatch_rows(x1, slot, xs, rows):
    n = x1.shape[0]
    return pl.pallas_call(
        functools.partial(_dispatch_kernel, rows=rows),
        grid=(n // rows,),
        in_specs=[pl.BlockSpec((rows * TOP_K,), lambda i: (i,), memory_space=pltpu.SMEM),
                  pl.BlockSpec((rows, D_MODEL), lambda i: (i, 0)), pl.BlockSpec(memory_space=pl.ANY)],
        out_specs=pl.BlockSpec(memory_space=pl.ANY),
        out_shape=jax.ShapeDtypeStruct(xs.shape, xs.dtype),
        scratch_shapes=[pltpu.SemaphoreType.DMA(())],
        input_output_aliases={2: 0},
        compiler_params=_params(("arbitrary",), 16),
        name="moe_dispatch",
    )(slot, x1, xs)


def _split_kernel(w_ref, p_ref, g_ref, u_ref):
    w = w_ref[...].astype(BF16)
    half = 2 * LANES
    for t in range(D_FF // half):
        y = _dot(w[:, 2 * half * t:2 * half * (t + 1)], p_ref[...])
        g_ref[:, half * t:half * (t + 1)] = y[:, :half].astype(BF16)
        u_ref[:, half * t:half * (t + 1)] = y[:, half:].astype(BF16)


def _split_gate_up(w_gate_up, rows):
    e, dm, f2 = w_gate_up.shape
    half = 2 * LANES
    r = lax.broadcasted_iota(jnp.int32, (2 * half, 2 * half), 0)
    c = lax.broadcasted_iota(jnp.int32, (2 * half, 2 * half), 1)
    perm = jnp.where(c < half, r == 2 * c, r == 2 * (c - half) + 1).astype(BF16)
    out = pl.BlockSpec((None, rows, f2 // 2), lambda ei, i: (ei, i, 0))
    return pl.pallas_call(
        _split_kernel,
        grid=(e, dm // rows),
        in_specs=[pl.BlockSpec((None, rows, f2), lambda ei, i: (ei, i, 0)),
                  pl.BlockSpec((2 * half, 2 * half), lambda ei, i: (0, 0))],
        out_specs=[out, out],
        out_shape=[jax.ShapeDtypeStruct((e, dm, f2 // 2), BF16)] * 2,
        compiler_params=_params(("parallel", "parallel"), 32),
        name="split_gate_up",
    )(w_gate_up, perm)


def _ffn_kernel(te_ref, nu_ref, xs_ref, wg_ref, wu_ref, bg_ref, bu_ref, wd_ref, bd_ref, o_ref):
    i = pl.program_id(0)

    @pl.when(i < nu_ref[0])
    def _():
        x = xs_ref[...].astype(BF16)
        g = jnp.minimum(_dot(x, wg_ref[...]) + bg_ref[...], SWIGLU_LIMIT)
        u = jnp.clip(_dot(x, wu_ref[...]) + bu_ref[...], -SWIGLU_LIMIT, SWIGLU_LIMIT)
        act = (u + 1.0) * g * jax.nn.sigmoid(SWIGLU_ALPHA * g)
        o_ref[...] = _dot(act.astype(BF16), wd_ref[...]) + bd_ref[...]

    @pl.when(i >= nu_ref[0])
    def _():
        o_ref[...] = jnp.zeros(o_ref.shape, F32)


def _expert_ffn(xs, tile_e, n_used, wg, wu, bg, bu, wd, bd, tm):
    n_slots = xs.shape[0]
    row = lambda i, te, nu: (i, 0)
    wsel = lambda i, te, nu: (te[i], 0, 0)
    grid_spec = pltpu.PrefetchScalarGridSpec(
        num_scalar_prefetch=2,
        grid=(n_slots // tm,),
        in_specs=[pl.BlockSpec((tm, D_MODEL), row),
                  pl.BlockSpec((None, D_MODEL, D_FF), wsel), pl.BlockSpec((None, D_MODEL, D_FF), wsel),
                  pl.BlockSpec((None, 1, D_FF), wsel), pl.BlockSpec((None, 1, D_FF), wsel),
                  pl.BlockSpec((None, D_FF, D_MODEL), wsel), pl.BlockSpec((None, 1, D_MODEL), wsel)],
        out_specs=pl.BlockSpec((tm, D_MODEL), row),
    )
    return pl.pallas_call(
        _ffn_kernel,
        grid_spec=grid_spec,
        out_shape=jax.ShapeDtypeStruct((n_slots, D_MODEL), F32),
        compiler_params=_params(("arbitrary",), 48),
        name="moe_ffn",
    )(tile_e, n_used, xs, wg, wu, bg, bu, wd, bd)


def _combine_kernel(slot_ref, next_slot_ref, x1_ref, gate_ref, y_hbm, g_ref, b_ref, o_ref, buf, sem, *, rows, alpha):
    i = pl.program_id(0)
    cur = lax.rem(i, 2)

    def issue(s_ref, b):
        def body(r, c):
            for k in range(TOP_K):
                pltpu.make_async_copy(y_hbm.at[pl.ds(s_ref[r * TOP_K + k], 1), :],
                                      buf.at[b, k, pl.ds(r, 1), :], sem.at[b]).start(priority=k % 2)
            return c

        lax.fori_loop(0, rows, body, 0)

    @pl.when(i == 0)
    def _():
        issue(slot_ref, 0)

    @pl.when(i + 1 < pl.num_programs(0))
    def _():
        issue(next_slot_ref, 1 - cur)

    for k in range(TOP_K):
        pltpu.make_async_copy(y_hbm.at[pl.ds(0, rows), :], buf.at[cur, k], sem.at[cur]).wait()
    gate = gate_ref[...]
    f = ((gate[:, 0:1] * buf[cur, 0] + gate[:, 1:2] * buf[cur, 1])
         + (gate[:, 2:3] * buf[cur, 2] + gate[:, 3:4] * buf[cur, 3]))
    z = alpha * x1_ref[...] + f
    mu = jnp.mean(z, axis=-1, keepdims=True)
    zc = z - mu
    var = jnp.mean(zc * zc, axis=-1, keepdims=True)
    o_ref[...] = zc * lax.rsqrt(var + LN_EPS) * g_ref[...] + b_ref[...]


def _combine(x1, gate, slot, yb, ln_g, ln_b, rows, alpha):
    n = x1.shape[0]
    steps = n // rows
    row = lambda i: (i, 0)
    vec = pl.BlockSpec((1, D_MODEL), lambda i: (0, 0))
    return pl.pallas_call(
        functools.partial(_combine_kernel, rows=rows, alpha=alpha),
        grid=(steps,),
        in_specs=[pl.BlockSpec((rows * TOP_K,), lambda i: (i,), memory_space=pltpu.SMEM),
                  pl.BlockSpec((rows * TOP_K,), lambda i: (jnp.minimum(i + 1, steps - 1),), memory_space=pltpu.SMEM),
                  pl.BlockSpec((rows, D_MODEL), row), pl.BlockSpec((rows, LANES), row),
                  pl.BlockSpec(memory_space=pl.ANY), vec, vec],
        out_specs=pl.BlockSpec((rows, D_MODEL), row),
        out_shape=jax.ShapeDtypeStruct((n, D_MODEL), F32),
        scratch_shapes=[pltpu.VMEM((2, TOP_K, rows, D_MODEL), F32), pltpu.SemaphoreType.DMA((2,))],
        compiler_params=_params(("arbitrary",), 32),
        name="moe_combine",
    )(slot, slot, x1, gate, yb, ln_g, ln_b)


def kernel(x_prompt, x_sample, cache_k, cache_v, state_conv, page_table, w_in, lambda_q1, lambda_k1,
           lambda_q2, lambda_k2, subln_g, w_attn_out, w_conv, b_conv, w_conv_out, w_o, ln1_g, ln1_b,
           w_router, b_router, w_gate_up, b_gate_up, w_down, b_down, ln2_g, ln2_b):
    bp, s, d = x_prompt.shape
    bd, t_new, _ = x_sample.shape
    depth = w_in.shape[0]
    n_phys = cache_k.shape[1]
    past = page_table.shape[1] * PAGE_SIZE
    alpha = (2.0 * depth) ** 0.25
    np_rows, ns_rows = bp * s, bd * t_new
    tm_p = 512
    tq = 512

    tab_p = _rope_tables(jnp.arange(s))
    tab_s = _rope_tables(jnp.tile(past + jnp.arange(t_new), bd))

    cache_k2 = jnp.transpose(cache_k, (0, 1, 3, 4, 5, 2)).reshape(depth * n_phys * QK_WIDTH, PAGE_SIZE)
    cache_v2 = cache_v.reshape(depth * n_phys * PAGE_SIZE * N_HEADS, V_DIM)
    hp = x_prompt.reshape(np_rows, d)
    hs = x_sample.reshape(ns_rows, d)
    outs = {k: [] for k in ("kp", "vp", "cp", "ks", "vs", "cs")}
    for l in range(depth):
        lam_init = 0.8 - 0.6 * math.exp(-0.3 * l)
        lp = jnp.concatenate([lambda_q1[l:l + 1], lambda_k1[l:l + 1], lambda_q2[l:l + 1], lambda_k2[l:l + 1]], axis=0)
        w_l = w_in[l]
        w_qkv = w_l[:, :QKV_COLS].astype(BF16)
        w_rest = w_l[:, QKV_COLS:].astype(BF16)
        w_r = jnp.pad(w_router[l], ((0, 0), (0, LANES - N_EXPERTS))).astype(BF16)
        b_r = jnp.pad(b_router[l], (0, LANES - N_EXPERTS)).reshape(1, LANES)
        weights = (w_rest, w_attn_out[l].astype(BF16), w_conv_out[l].astype(BF16), w_o[l].astype(BF16), w_r)
        vecs = (subln_g[l].reshape(1, V_DIM), w_conv[l],
                b_conv[l].reshape(1, CONV_CH), ln1_g[l].reshape(1, d), ln1_b[l].reshape(1, d), b_r)

        q1, q2, kf, kb, vf, vt = _qkv_proj(hp, w_qkv, tab_p, tm_p)
        shp = (bp, s, QK_WIDTH)
        o_p = _flash_attention(lp, q1.reshape(shp), q2.reshape(shp), kb.reshape(shp), vt, lam_init, tq)
        x1_p, ti_p, tg_p, nc_p = _post_block(
            hp, o_p.reshape(np_rows, ATTN_WIDTH), weights, vecs, tm=tm_p // 2, tiles_per_seq=s // (tm_p // 2),
            conv_state=jnp.zeros((bp, CONV_K - 1, CONV_CH), F32), alpha=alpha, lam_init=lam_init)
        outs["kp"].append(kf.reshape(bp, s, N_HEADS, 2, HEAD_DIM))
        outs["vp"].append(vf.reshape(bp, s, N_HEADS, V_DIM))
        outs["cp"].append(nc_p)

        q1s, q2s, kfs, _, vfs, _ = _qkv_proj(hs, w_qkv, tab_s, ns_rows)
        seq3 = lambda a: a.reshape(bd, t_new, QK_WIDTH)
        pad_page = lambda a: jnp.pad(seq3(a), ((0, 0), (0, PAGE_SIZE - t_new), (0, 0)))
        o_s = _paged_attention(
            lp, page_table + l * n_phys, seq3(q1s).astype(F32), seq3(q2s).astype(F32), pad_page(kfs), pad_page(vfs),
            cache_k2, cache_v2, lam_init)
        st = state_conv[l]
        zero = jnp.zeros((bd, 1, CONV_CH), F32)
        hist1 = jnp.concatenate([st[:, 1:2]] + [zero] * (t_new - 1), axis=1).reshape(ns_rows, CONV_CH)
        hist2 = jnp.concatenate([st[:, 0:1], st[:, 1:2]] + [zero] * (t_new - 2), axis=1).reshape(ns_rows, CONV_CH)
        x1_s, ti_s, tg_s, u_s = _post_block(
            hs, o_s.reshape(ns_rows, ATTN_WIDTH), weights, vecs, tm=ns_rows, hist=(hist1, hist2), t_short=t_new,
            alpha=alpha, lam_init=lam_init)
        outs["ks"].append(kfs.reshape(bd, t_new, N_HEADS, 2, HEAD_DIM))
        outs["vs"].append(vfs.reshape(bd, t_new, N_HEADS, V_DIM))
        outs["cs"].append(u_s.reshape(bd, t_new, CONV_CH)[:, t_new - (CONV_K - 1):])

        top_i = jnp.concatenate([ti_p[:, :TOP_K], ti_s[:, :TOP_K]], axis=0)
        slot, tile_e, n_used, n_slots = _route(top_i, FFN_TILE)
        slot_p, slot_s = slot[:np_rows * TOP_K], slot[np_rows * TOP_K:]
        xs = _dispatch_rows(x1_p, slot_p, jnp.zeros((n_slots, d), F32), 1024)
        xs = _dispatch_rows(x1_s, slot_s, xs, ns_rows)
        w_gate, w_up = _split_gate_up(w_gate_up[l], 512)
        bgu = b_gate_up[l]
        yb = _expert_ffn(
            xs, tile_e, n_used, w_gate, w_up,
            bgu[:, None, 0::2], bgu[:, None, 1::2], w_down[l].astype(BF16), b_down[l][:, None, :], FFN_TILE)
        g2, b2 = ln2_g[l].reshape(1, d), ln2_b[l].reshape(1, d)
        hp = _combine(x1_p, tg_p, slot_p, yb, g2, b2, 256, alpha)
        hs = _combine(x1_s, tg_s, slot_s, yb, g2, b2, ns_rows, alpha)

    return (hp.reshape(bp, s, d), hs.reshape(bd, t_new, d),
            jnp.stack(outs["kp"]), jnp.stack(outs["vp"]), jnp.stack(outs["cp"]),
            jnp.stack(outs["ks"]), jnp.stack(outs["vs"]), jnp.stack(outs["cs"]))
```
